```python
import math
import jax, jax.numpy as jnp
from jax import lax
import numpy as np

D_MODEL = 2048
BATCH = 2
SEQ = 16384
DEPTH = 1

HEAD_DIM = 128
DN_HEADS = D_MODEL // (2 * HEAD_DIM)
NSA_HEADS = D_MODEL // (2 * HEAD_DIM)
NSA_KV_GROUPS = 2
NSA_HPG = NSA_HEADS // NSA_KV_GROUPS
DN_KEY = DN_HEADS * HEAD_DIM
NSA_Q = NSA_HEADS * HEAD_DIM
KVW = NSA_KV_GROUPS * HEAD_DIM
MIX_WIDTH = DN_KEY + NSA_Q
CONV_WIDTH = 4
DN_CHUNK = 64
CMP_LEN = 32
CMP_STRIDE = 16
CMP_HIDDEN = HEAD_DIM
SEL_LEN = 64
SEL_COUNT = 16
WINDOW = 512
Q_BLOCK = 128
N_GROUPS = 8
EXPERTS_PER_GROUP = 8
N_EXPERTS = N_GROUPS * EXPERTS_PER_GROUP
TOP_K = 2
D_EXPERT = 512
MOE_BLOCK = 128
NORM_EPS = 1e-6
NEG_INF = -1e30
SEL_FORCE = 1e9
IN_SPLITS = (3 * DN_KEY, DN_KEY, DN_HEADS, DN_HEADS, NSA_Q, KVW, KVW, KVW, KVW, KVW, KVW, 3 * NSA_HEADS)
D_IN_PROJ = 4 * DN_KEY + 2 * DN_HEADS + NSA_Q + 6 * KVW + 3 * NSA_HEADS

kernel_name = "hymba_deltanet_nsa_hiermoe"


def rms_norm(x, g):
    xf = x.astype(jnp.float32)
    y = xf * lax.rsqrt(jnp.mean(xf * xf, axis=-1, keepdims=True) + NORM_EPS)
    return (y * g.astype(jnp.float32)).astype(x.dtype)


def l2_normalize(x):
    xf = x.astype(jnp.float32)
    return xf * lax.rsqrt(jnp.sum(xf * xf, axis=-1, keepdims=True) + NORM_EPS)


def causal_conv(x, w):
    S = x.shape[1]
    xp = jnp.pad(x, ((0, 0), (CONV_WIDTH - 1, 0), (0, 0)))
    out = xp[:, 0:S] * w[0]
    for i in range(1, CONV_WIDTH):
        out = out + xp[:, i:i + S] * w[i]
    return out


def masked_softmax(s, mask):
    s = jnp.where(mask, s.astype(jnp.float32), NEG_INF)
    e = jnp.where(mask, jnp.exp(s - jnp.max(s, axis=-1, keepdims=True)), 0.0)
    return e / jnp.maximum(jnp.sum(e, axis=-1, keepdims=True), 1e-30)


def alibi_slopes(n_heads):
    return (2.0 ** (-8.0 * (np.arange(n_heads) + 1) / n_heads)).astype(np.float32)


def gated_delta_rule(q, k, v, log_decay, beta):
    B, S, H, Dk = q.shape
    Dv = v.shape[-1]
    C = DN_CHUNK
    n = S // C

    def chunks(t):
        return jnp.moveaxis(t.reshape((B, n, C, H) + t.shape[3:]), 3, 1)

    q = chunks(q) * (Dk ** -0.5)
    k = chunks(k)
    v = chunks(v)
    beta = chunks(beta)
    gcum = jnp.cumsum(chunks(log_decay), axis=-1)
    idx = jnp.arange(C)
    causal = idx[:, None] >= idx[None, :]
    strict = idx[:, None] > idx[None, :]
    decay = jnp.exp(jnp.where(causal, gcum[..., :, None] - gcum[..., None, :], -jnp.inf))
    k_beta = k * beta[..., None]
    v_beta = v * beta[..., None]
    kk = jnp.einsum('bhnid,bhnjd->bhnij', k_beta, k) * decay
    tri = jnp.where(strict, kk, 0.0) + jnp.eye(C, dtype=kk.dtype)
    rhs = jnp.concatenate([v_beta, k_beta * jnp.exp(gcum)[..., None]], axis=-1)
    sol = lax.linalg.triangular_solve(tri, rhs, left_side=True, lower=True, unit_diagonal=True)
    u, w = sol[..., :Dv], sol[..., Dv:]
    qk = jnp.einsum('bhnid,bhnjd->bhnij', q, k) * decay
    q_dec = q * jnp.exp(gcum)[..., None]
    g_last = gcum[..., -1]
    k_dec = k * jnp.exp(g_last[..., None] - gcum)[..., None]

    def step(state, xs):
        u_i, w_i, qd_i, qk_i, kd_i, gl_i = xs
        v_new = u_i - jnp.einsum('bhcd,bhde->bhce', w_i, state)
        o_i = jnp.einsum('bhcd,bhde->bhce', qd_i, state) + jnp.einsum('bhij,bhje->bhie', qk_i, v_new)
        state = state * jnp.exp(gl_i)[..., None, None] + jnp.einsum('bhcd,bhce->bhde', kd_i, v_new)
        return state, o_i

    xs = tuple(jnp.moveaxis(t, 2, 0) for t in (u, w, q_dec, qk, k_dec, g_last))
    state0 = jnp.zeros((B, H, Dk, Dv), jnp.float32)
    _, o = lax.scan(step, state0, xs)
    return jnp.transpose(o, (1, 0, 3, 2, 4)).reshape(B, S, H, Dv)


def deltanet_heads(dn_qkv, dn_z, dn_b, dn_a, conv_w, dt_bias, a_log, norm_g):
    B, S, _ = dn_qkv.shape
    qkv = jax.nn.silu(causal_conv(dn_qkv, conv_w))
    q, k, v = jnp.split(qkv, 3, axis=-1)
    q = l2_normalize(q.reshape(B, S, DN_HEADS, HEAD_DIM))
    k = l2_normalize(k.reshape(B, S, DN_HEADS, HEAD_DIM))
    v = v.reshape(B, S, DN_HEADS, HEAD_DIM).astype(jnp.float32)
    beta = jax.nn.sigmoid(dn_b.astype(jnp.float32))
    log_decay = -jnp.exp(a_log.astype(jnp.float32)) * jax.nn.softplus(
        dn_a.astype(jnp.float32) + dt_bias.astype(jnp.float32))
    o = gated_delta_rule(q, k, v, log_decay, beta)
    z = dn_z.reshape(B, S, DN_HEADS, HEAD_DIM).astype(jnp.float32)
    o = rms_norm(o, norm_g) * jax.nn.silu(z)
    return o.reshape(B, S, DN_KEY).astype(dn_qkv.dtype)


def compress_tokens(kv, pos_emb, w1, w2):
    S = kv.shape[1]
    n_cmp = (S - CMP_LEN) // CMP_STRIDE + 1
    idx = np.arange(n_cmp)[:, None] * CMP_STRIDE + np.arange(CMP_LEN)[None, :]
    blocks = kv[:, idx] + pos_emb[:, None, :]
    hid = jax.nn.gelu(jnp.einsum('bnlgd,ldh->bngh', blocks, w1))
    return jnp.einsum('bngh,he->bnge', hid, w2)


def cmp_to_sel_matrix(n_cmp, n_sel):
    r = SEL_LEN // CMP_STRIDE
    c = CMP_LEN // CMP_STRIDE
    off = np.arange(n_cmp)[:, None] - r * np.arange(n_sel)[None, :] + (c - 1)
    cnt = np.minimum(c - 1, off) - np.maximum(0, off - r + 1) + 1
    return np.clip(cnt, 0, None).astype(np.float32)


def nsa_heads(nsa_q, k_cmp, v_cmp, k_sel, v_sel, k_win, v_win, nsa_gate,
              cmp_pos_k, cmp_w1_k, cmp_w2_k, cmp_pos_v, cmp_w1_v, cmp_w2_v, norm_g):
    B, S, _ = nsa_q.shape
    G, HPG, D = NSA_KV_GROUPS, NSA_HPG, HEAD_DIM
    scale = D ** -0.5
    kv_shape = (B, S, G, D)
    kc = compress_tokens(k_cmp.reshape(kv_shape), cmp_pos_k, cmp_w1_k, cmp_w2_k)
    vc = compress_tokens(v_cmp.reshape(kv_shape), cmp_pos_v, cmp_w1_v, cmp_w2_v)
    n_cmp = kc.shape[1]
    cmp_pos = jnp.asarray(np.arange(n_cmp) * CMP_STRIDE + CMP_LEN - 1, jnp.int32)
    n_sel = S // SEL_LEN
    n_pick = min(SEL_COUNT, n_sel)
    agg = jnp.asarray(cmp_to_sel_matrix(n_cmp, n_sel))
    ks_blocks = jnp.transpose(k_sel.reshape(B, n_sel, SEL_LEN, G, D), (0, 3, 1, 2, 4))
    vs_blocks = jnp.transpose(v_sel.reshape(B, n_sel, SEL_LEN, G, D), (0, 3, 1, 2, 4))
    kw_pad = jnp.pad(k_win.reshape(kv_shape), ((0, 0), (WINDOW, 0), (0, 0), (0, 0)))
    vw_pad = jnp.pad(v_win.reshape(kv_shape), ((0, 0), (WINDOW, 0), (0, 0), (0, 0)))
    slopes = jnp.asarray(alibi_slopes(NSA_HEADS).reshape(G, HPG))[None, :, :, None, None]
    gates = jax.nn.sigmoid(nsa_gate.astype(jnp.float32))
    nq = S // Q_BLOCK
    q_blocks = jnp.moveaxis(nsa_q.reshape(B, nq, Q_BLOCK, G, HPG, D), 1, 0)
    g_blocks = jnp.moveaxis(gates.reshape(B, nq, Q_BLOCK, G, HPG, 3), 1, 0)
    bi = jnp.arange(B)[:, None, None]
    gi = jnp.arange(G)[None, :, None]
    sel_id = jnp.arange(n_sel)

    def alibi_attn(scores, dist, mask):
        return masked_softmax(scores * scale - slopes * dist.astype(jnp.float32), mask)

    def block_fn(args):
        qb, gb, blk = args
        t = blk * Q_BLOCK + jnp.arange(Q_BLOCK)
        s_c = jnp.einsum('bqghd,bngd->bghqn', qb, kc)
        dist_c = t[:, None] - cmp_pos[None, :]
        p_c = alibi_attn(s_c, dist_c, dist_c >= 0)
        o_c = jnp.einsum('bghqn,bngd->bqghd', p_c.astype(vc.dtype), vc)
        imp = jnp.einsum('bghqn,ns->bgqs', p_c, agg)
        cur = t // SEL_LEN
        forced = (sel_id == 0) | (sel_id == cur[:, None]) | (sel_id == cur[:, None] - 1)
        valid = sel_id * SEL_LEN <= t[:, None]
        imp = jnp.where(forced, SEL_FORCE, jnp.where(valid, imp, NEG_INF))
        _, pick = lax.top_k(imp, n_pick)
        flat = pick.reshape(B, G, Q_BLOCK * n_pick)
        k_s = ks_blocks[bi, gi, flat].reshape(B, G, Q_BLOCK, n_pick * SEL_LEN, D)
        v_s = vs_blocks[bi, gi, flat].reshape(B, G, Q_BLOCK, n_pick * SEL_LEN, D)
        pos_s = (pick[..., None] * SEL_LEN + jnp.arange(SEL_LEN)).reshape(B, G, Q_BLOCK, n_pick * SEL_LEN)
        dist_s = (t[:, None] - pos_s)[:, :, None]
        s_s = jnp.einsum('bqghd,bgqkd->bghqk', qb, k_s)
        p_s = alibi_attn(s_s, dist_s, dist_s >= 0)
        o_s = jnp.einsum('bghqk,bgqkd->bqghd', p_s.astype(v_s.dtype), v_s)
        start = blk * Q_BLOCK
        k_w = lax.dynamic_slice_in_dim(kw_pad, start, WINDOW + Q_BLOCK, axis=1)
        v_w = lax.dynamic_slice_in_dim(vw_pad, start, WINDOW + Q_BLOCK, axis=1)
        pos_w = start - WINDOW + jnp.arange(WINDOW + Q_BLOCK)
        dist_w = t[:, None] - pos_w[None, :]
        mask_w = (dist_w >= 0) & (dist_w < WINDOW) & (pos_w[None, :] >= 0)
        s_w = jnp.einsum('bqghd,bkgd->bghqk', qb, k_w)
        p_w = alibi_attn(s_w, dist_w, mask_w)
        o_w = jnp.einsum('bghqk,bkgd->bqghd', p_w.astype(v_w.dtype), v_w)
        return gb[..., 0:1] * o_c + gb[..., 1:2] * o_s + gb[..., 2:3] * o_w

    o = lax.map(block_fn, (q_blocks, g_blocks, jnp.arange(nq)))
    o = jnp.moveaxis(o, 0, 1).reshape(B, S, NSA_HEADS, D)
    return rms_norm(o, norm_g).reshape(B, S, NSA_Q).astype(nsa_q.dtype)


def hierarchical_moe(h, w_group, b_group, w_router, b_router, w_gate, w_up, w_down):
    B, S, D = h.shape
    T = B * S
    ht = h.reshape(T, D)
    tok_ids = jnp.arange(T)
    group_logits = jnp.einsum('td,dg->tg', ht, w_group).astype(jnp.float32)
    p_group = jax.nn.softmax(group_logits, axis=-1)
    g_sel = jnp.argmax(group_logits + b_group.astype(jnp.float32), axis=-1)
    exp_logits = jnp.einsum('td,de->te', ht, w_router).astype(jnp.float32).reshape(T, N_GROUPS, EXPERTS_PER_GROUP)
    el = exp_logits[tok_ids, g_sel]
    bias = b_router.astype(jnp.float32).reshape(N_GROUPS, EXPERTS_PER_GROUP)[g_sel]
    _, top_idx = lax.top_k(el + bias, TOP_K)
    w_top = jax.nn.softmax(jnp.take_along_axis(el, top_idx, axis=-1), axis=-1)
    gate = p_group[tok_ids, g_sel][:, None] * w_top
    expert = g_sel[:, None] * EXPERTS_PER_GROUP + top_idx
    A = T * TOP_K
    e_flat = expert.reshape(A)
    tok_flat = jnp.repeat(tok_ids, TOP_K)
    w_flat = gate.reshape(A)
    order = jnp.argsort(e_flat)
    e_s, tok_s, w_s = e_flat[order], tok_flat[order], w_flat[order]
    counts = jnp.zeros((N_EXPERTS,), jnp.int32).at[e_flat].add(1)
    starts = jnp.cumsum(counts) - counts
    padded = (counts + MOE_BLOCK - 1) // MOE_BLOCK * MOE_BLOCK
    pends = jnp.cumsum(padded)
    pstarts = pends - padded
    dest = pstarts[e_s] + (jnp.arange(A) - starts[e_s])
    n_blocks = (A + MOE_BLOCK - 1) // MOE_BLOCK + N_EXPERTS
    P = n_blocks * MOE_BLOCK
    row_tok = jnp.full((P,), T, jnp.int32).at[dest].set(tok_s)
    row_w = jnp.zeros((P,), jnp.float32).at[dest].set(w_s)
    block_start = jnp.arange(n_blocks) * MOE_BLOCK
    block_expert = jnp.minimum(jnp.sum(block_start[:, None] >= pends[None, :], axis=1), N_EXPERTS - 1)
    ht_pad = jnp.concatenate([ht, jnp.zeros((1, D), ht.dtype)], axis=0)

    def run_block(args):
        toks, ws, e = args
        xb = ht_pad[toks]
        hid = jax.nn.silu(xb @ w_gate[e]) * (xb @ w_up[e])
        return ((hid @ w_down[e]) * ws[:, None]).astype(ht.dtype)

    y = lax.map(run_block, (row_tok.reshape(n_blocks, MOE_BLOCK), row_w.reshape(n_blocks, MOE_BLOCK), block_expert))
    out = jnp.zeros((T + 1, D), ht.dtype).at[row_tok].add(y.reshape(P, D))
    return out[:T].reshape(B, S, D)


def setup_inputs(seed: int = 0) -> dict:
    key = jax.random.key(seed)
    ks = jax.random.split(key, 26)
    f32 = jnp.float32
    L = DEPTH

    def nrm(k, shape, fan_in):
        return jax.random.normal(k, shape, f32) * (fan_in ** -0.5)

    def gain(k, shape):
        return 1.0 + 0.02 * jax.random.normal(k, shape, f32)

    dt = jnp.exp(jax.random.uniform(ks[4], (L, DN_HEADS), f32, math.log(1e-3), math.log(1e-1)))
    return {
        "x": jax.random.normal(ks[0], (BATCH, SEQ, D_MODEL), f32),
        "norm_mix_g": gain(ks[1], (L, D_MODEL)),
        "w_in": nrm(ks[2], (L, D_MODEL, D_IN_PROJ), D_MODEL),
        "conv_w": nrm(ks[3], (L, CONV_WIDTH, 3 * DN_KEY), CONV_WIDTH),
        "dt_bias": dt + jnp.log(-jnp.expm1(-dt)),
        "a_log": jnp.log(jax.random.uniform(ks[5], (L, DN_HEADS), f32, 1.0, 16.0)),
        "dn_norm_g": gain(ks[6], (L, HEAD_DIM)),
        "cmp_pos_k": 0.02 * jax.random.normal(ks[7], (L, CMP_LEN, HEAD_DIM), f32),
        "cmp_w1_k": nrm(ks[8], (L, CMP_LEN, HEAD_DIM, CMP_HIDDEN), CMP_LEN * HEAD_DIM),
        "cmp_w2_k": nrm(ks[9], (L, CMP_HIDDEN, HEAD_DIM), CMP_HIDDEN),
        "cmp_pos_v": 0.02 * jax.random.normal(ks[10], (L, CMP_LEN, HEAD_DIM), f32),
        "cmp_w1_v": nrm(ks[11], (L, CMP_LEN, HEAD_DIM, CMP_HIDDEN), CMP_LEN * HEAD_DIM),
        "cmp_w2_v": nrm(ks[12], (L, CMP_HIDDEN, HEAD_DIM), CMP_HIDDEN),
        "nsa_norm_g": gain(ks[13], (L, HEAD_DIM)),
        "w_out": nrm(ks[14], (L, MIX_WIDTH, D_MODEL), MIX_WIDTH),
        "norm_ffn_g": gain(ks[15], (L, D_MODEL)),
        "w_group": nrm(ks[16], (L, D_MODEL, N_GROUPS), D_MODEL),
        "b_group": 0.01 * jax.random.normal(ks[17], (L, N_GROUPS), f32),
        "w_router": nrm(ks[18], (L, D_MODEL, N_EXPERTS), D_MODEL),
        "b_router": 0.01 * jax.random.normal(ks[19], (L, N_EXPERTS), f32),
        "w_gate": nrm(ks[20], (L, N_EXPERTS, D_MODEL, D_EXPERT), D_MODEL),
        "w_up": nrm(ks[21], (L, N_EXPERTS, D_MODEL, D_EXPERT), D_MODEL),
        "w_down": nrm(ks[22], (L, N_EXPERTS, D_EXPERT, D_MODEL), D_EXPERT),
        "norm_final_g": gain(ks[23], (D_MODEL,)),
    }


def reference(x, norm_mix_g, w_in, conv_w, dt_bias, a_log, dn_norm_g,
              cmp_pos_k, cmp_w1_k, cmp_w2_k, cmp_pos_v, cmp_w1_v, cmp_w2_v, nsa_norm_g,
              w_out, norm_ffn_g, w_group, b_group, w_router, b_router,
              w_gate, w_up, w_down, norm_final_g):
    cuts = [int(c) for c in np.cumsum(IN_SPLITS)[:-1]]
    for l in range(DEPTH):
        h = rms_norm(x, norm_mix_g[l])
        proj = jnp.einsum('bsd,de->bse', h, w_in[l])
        (dn_qkv, dn_z, dn_b, dn_a, nsa_q, k_cmp, v_cmp, k_sel, v_sel,
         k_win, v_win, nsa_gate) = jnp.split(proj, cuts, axis=-1)
        y_dn = deltanet_heads(dn_qkv, dn_z, dn_b, dn_a, conv_w[l], dt_bias[l], a_log[l], dn_norm_g[l])
        y_nsa = nsa_heads(nsa_q, k_cmp, v_cmp, k_sel, v_sel, k_win, v_win, nsa_gate,
                          cmp_pos_k[l], cmp_w1_k[l], cmp_w2_k[l], cmp_pos_v[l], cmp_w1_v[l], cmp_w2_v[l],
                          nsa_norm_g[l])
        x = x + jnp.einsum('bse,ed->bsd', jnp.concatenate([y_dn, y_nsa], axis=-1), w_out[l])
        x = x + hierarchical_moe(rms_norm(x, norm_ffn_g[l]), w_group[l], b_group[l], w_router[l],
                                 b_router[l], w_gate[l], w_up[l], w_down[l])
    return rms_norm(x, norm_final_g)
```

```python
import functools
import math

import numpy as np
import jax
import jax.numpy as jnp
from jax import lax
from jax.experimental import pallas as pl
from jax.experimental.pallas import tpu as pltpu

F32 = jnp.float32
BF16 = jnp.bfloat16
I32 = jnp.int32

LANE = 128
HEAD_DIM = 128
DN_HEADS = 8
NSA_HEADS = 8
NSA_KV_GROUPS = 2
NSA_HPG = NSA_HEADS // NSA_KV_GROUPS
CONV_WIDTH = 4
DN_CHUNK = 64
CMP_LEN = 32
CMP_STRIDE = 16
SEL_LEN = 64
SEL_COUNT = 16
WINDOW = 512
Q_BLOCK = 128
N_GROUPS = 8
EXPERTS_PER_GROUP = 8
N_EXPERTS = N_GROUPS * EXPERTS_PER_GROUP
NORM_EPS = 1e-6
NEG_INF = -1e30
SEL_FORCE = 1e9
PICKED = -3e38

SLAB_DN_Q, SLAB_DN_K, SLAB_DN_V, SLAB_DN_Z = 0, 8, 16, 24
SLAB_NSA_Q = 32
SLAB_K_CMP, SLAB_V_CMP, SLAB_K_SEL, SLAB_V_SEL, SLAB_K_WIN, SLAB_V_WIN = 40, 42, 44, 46, 48, 50
SLAB_SMALL = 52
N_SLABS = 54
LANE_DN_B, LANE_DN_A, LANE_GATE = 0, 8, 16

KEY_TILE = 256
MOE_ROWS = 256

VMEM_LIMIT = 56 * 1024 * 1024


def _params(sem, vmem=VMEM_LIMIT):
    return pltpu.CompilerParams(dimension_semantics=sem, vmem_limit_bytes=vmem)


def _dot(a, b):
    return jnp.dot(a, b, preferred_element_type=F32)


def _dot_nt(a, b):
    return lax.dot_general(a, b, (((1,), (1,)), ((), ())), preferred_element_type=F32)


def _dot_exact(a, b):
    return jnp.dot(a, b, preferred_element_type=F32, precision=lax.Precision.HIGHEST)


def _sigmoid(x):
    return 1.0 / (1.0 + jnp.exp(-x))


def _silu(x):
    return x * _sigmoid(x)


def _pick_lane(x, lane_ids, idx):
    return jnp.sum(jnp.where(lane_ids == idx, x, 0.0), axis=1, keepdims=True)


def _in_proj_kernel(x_ref, g_ref, w_ref, o_ref, h_scr):
    @pl.when(pl.program_id(1) == 0)
    def _():
        xf = x_ref[...]
        ms = jnp.mean(xf * xf, axis=-1, keepdims=True)
        h_scr[...] = (xf * lax.rsqrt(ms + NORM_EPS) * g_ref[...]).astype(BF16)

    acc = _dot(h_scr[...], w_ref[...])
    for k in range(o_ref.shape[0]):
        o_ref[k] = acc[:, k * LANE:(k + 1) * LANE]


def _in_proj(x2d, g, w_slabs, tm=512, tn=768):
    T, D = x2d.shape
    NP = w_slabs.shape[1]
    ns = tn // LANE
    return pl.pallas_call(
        _in_proj_kernel,
        out_shape=jax.ShapeDtypeStruct((NP // LANE, T, LANE), F32),
        grid=(T // tm, NP // tn),
        in_specs=[
            pl.BlockSpec((tm, D), lambda i, j: (i, 0)),
            pl.BlockSpec((1, D), lambda i, j: (0, 0)),
            pl.BlockSpec((D, tn), lambda i, j: (0, j)),
        ],
        out_specs=pl.BlockSpec((ns, tm, LANE), lambda i, j: (j, i, 0)),
        scratch_shapes=[pltpu.VMEM((tm, D), BF16)],
        compiler_params=_params(("parallel", "arbitrary")),
        name="in_proj",
    )(x2d, g, w_slabs)


def _deltanet_kernel(q_ref, k_ref, v_ref, z_ref, sm_ref, cwq_ref, cwk_ref, cwv_ref,
                     alog_ref, dtb_ref, ng_ref, o_ref, ext_scr, s_scr):
    L = q_ref.shape[1]
    C = DN_CHUNK
    h = pl.program_id(1)

    @pl.when(pl.program_id(2) == 0)
    def _():
        ext_scr[:, 0:8, :] = jnp.zeros((3, 8, LANE), F32)
        s_scr[...] = jnp.zeros_like(s_scr)

    def conv_silu(idx, x_ref, cw_ref):
        ext_scr[idx, 8:, :] = x_ref[0]
        w = cw_ref[0]
        acc = ext_scr[idx, pl.ds(8 - (CONV_WIDTH - 1), L), :] * w[0:1]
        for j in range(1, CONV_WIDTH):
            acc = acc + ext_scr[idx, pl.ds(8 - (CONV_WIDTH - 1) + j, L), :] * w[j:j + 1]
        ext_scr[idx, 0:8, :] = ext_scr[idx, L:L + 8, :]
        return _silu(acc)

    def l2n(x):
        return x * lax.rsqrt(jnp.sum(x * x, axis=-1, keepdims=True) + NORM_EPS)

    q = l2n(conv_silu(0, q_ref, cwq_ref)) * (HEAD_DIM ** -0.5)
    k = l2n(conv_silu(1, k_ref, cwk_ref))
    v = conv_silu(2, v_ref, cwv_ref)

    sm = sm_ref[0]
    lane = lax.broadcasted_iota(I32, (1, LANE), 1)
    beta = _pick_lane(_sigmoid(sm), lane, LANE_DN_B + h)
    xa = sm + dtb_ref[...]
    softplus = jnp.maximum(xa, 0.0) + jnp.log1p(jnp.exp(-jnp.abs(xa)))
    ld = -jnp.exp(alog_ref[...]) * softplus

    ri = lax.broadcasted_iota(I32, (L, L), 0)
    ci = lax.broadcasted_iota(I32, (L, L), 1)
    same = (ri // C) == (ci // C)
    causal = same & (ci <= ri)
    gc_slab = _dot_exact(jnp.where(causal, 1.0, 0.0), ld)
    gl_slab = _dot_exact(jnp.where(same, 1.0, 0.0), ld)
    gcol = _pick_lane(gc_slab, lane, LANE_DN_A + h)
    glast = _pick_lane(gl_slab, lane, LANE_DN_A + h)
    sub = lax.broadcasted_iota(I32, (LANE, 1), 0)
    grow = jnp.sum(jnp.where(sub == LANE_DN_A + h, gc_slab.T, 0.0), axis=0, keepdims=True)

    decay = jnp.where(causal, jnp.exp(jnp.where(causal, gcol - grow, 0.0)), 0.0)
    egc = jnp.exp(gcol)
    kb = k * beta
    kt = k.astype(BF16)
    kk = _dot_nt(kb.astype(BF16), kt) * decay
    lm = jnp.where(ci < ri, kk, 0.0)
    eye = jnp.where(ri == ci, 1.0, 0.0)
    tinv = eye - lm
    pw = lm
    for _ in range(int(math.log2(C)) - 1):
        pwb = pw.astype(BF16)
        pw = _dot(pwb, pwb)
        tinv = tinv + _dot(tinv.astype(BF16), pw.astype(BF16))
    tb = tinv.astype(BF16)
    u = _dot(tb, (v * beta).astype(BF16))
    w = _dot(tb, (kb * egc).astype(BF16))
    qk = _dot_nt(q.astype(BF16), kt) * decay
    q_dec = (q * egc).astype(BF16)
    k_dec_t = (k * jnp.exp(glast - gcol)).T

    outs = []
    for c in range(L // C):
        sl = slice(c * C, (c + 1) * C)
        s_old = s_scr[...]
        sb = s_old.astype(BF16)
        v_new = u[sl] - _dot(w[sl].astype(BF16), sb)
        vnb = v_new.astype(BF16)
        outs.append(_dot(q_dec[sl], sb) + _dot(qk[sl, sl].astype(BF16), vnb))
        egl = jnp.exp(glast[c * C:c * C + 1, :])
        s_scr[...] = s_old * egl + _dot(k_dec_t[:, sl].astype(BF16), vnb)
    o = jnp.concatenate(outs, axis=0)

    on = o * lax.rsqrt(jnp.mean(o * o, axis=-1, keepdims=True) + NORM_EPS) * ng_ref[...]
    o_ref[...] = (on * _silu(z_ref[0])).astype(o_ref.dtype)


def _deltanet(P, B, S, conv_w, dt_bias, a_log, norm_g, L=256):
    T = B * S
    nL = S // L
    cw = conv_w.reshape(CONV_WIDTH, 3 * DN_HEADS, LANE).transpose(1, 0, 2)
    alog = jnp.zeros((1, LANE), F32).at[0, LANE_DN_A:LANE_DN_A + DN_HEADS].set(a_log)
    dtb = jnp.zeros((1, LANE), F32).at[0, LANE_DN_A:LANE_DN_A + DN_HEADS].set(dt_bias)

    def slab(base):
        return pl.BlockSpec((1, L, LANE), lambda b, h, i: (base + h, b * nL + i, 0))

    def cws(base):
        return pl.BlockSpec((1, CONV_WIDTH, LANE), lambda b, h, i: (base + h, 0, 0))

    vec = pl.BlockSpec((1, LANE), lambda b, h, i: (0, 0))
    return pl.pallas_call(
        _deltanet_kernel,
        out_shape=jax.ShapeDtypeStruct((T, DN_HEADS * HEAD_DIM), BF16),
        grid=(B, DN_HEADS, nL),
        in_specs=[
            slab(SLAB_DN_Q), slab(SLAB_DN_K), slab(SLAB_DN_V), slab(SLAB_DN_Z),
            pl.BlockSpec((1, L, LANE), lambda b, h, i: (SLAB_SMALL, b * nL + i, 0)),
            cws(0), cws(DN_HEADS), cws(2 * DN_HEADS), vec, vec, vec,
        ],
        out_specs=pl.BlockSpec((L, LANE), lambda b, h, i: (b * nL + i, h)),
        scratch_shapes=[pltpu.VMEM((3, L + 8, LANE), F32), pltpu.VMEM((HEAD_DIM, HEAD_DIM), F32)],
        compiler_params=_params(("parallel", "parallel", "arbitrary")),
        name="deltanet",
    )(P, P, P, P, P, cw, cw, cw, alog, dtb, norm_g.reshape(1, LANE))


def _gelu_tanh(x):
    return 0.5 * x * (1.0 + jnp.tanh(math.sqrt(2.0 / math.pi) * (x + 0.044715 * (x * x * x))))


def _compress_kernel(x_ref, pos_ref, w1_ref, w2_ref, o_ref):
    n = o_ref.shape[3]
    half = CMP_LEN // 2
    y1 = jnp.zeros((n, HEAD_DIM), F32)
    y2 = jnp.zeros((n, HEAD_DIM), F32)
    for l in range(half):
        xl = x_ref[0, pl.ds(l, n, stride=CMP_STRIDE), :]
        y1 = y1 + _dot((xl + pos_ref[0, l:l + 1, :]).astype(BF16), w1_ref[0, l].astype(BF16))
        y2 = y2 + _dot((xl + pos_ref[0, half + l:half + l + 1, :]).astype(BF16),
                       w1_ref[0, half + l].astype(BF16))
    hid = _gelu_tanh(y1 + pltpu.roll(y2, n - 1, 0))
    row = lax.broadcasted_iota(I32, (n, 1), 0)
    out = _dot(hid.astype(BF16), w2_ref[0].astype(BF16))
    o_ref[0, 0, 0] = jnp.where(row < n - 1, out, 0.0).astype(o_ref.dtype)


def _compress(P, B, S, pos, w1, w2):
    n = S // CMP_STRIDE
    G = NSA_KV_GROUPS
    return pl.pallas_call(
        _compress_kernel,
        out_shape=jax.ShapeDtypeStruct((2, B, G, n, HEAD_DIM), BF16),
        grid=(2, B, G),
        in_specs=[
            pl.BlockSpec((1, S, LANE), lambda c, b, g: (SLAB_K_CMP + 2 * c + g, b, 0)),
            pl.BlockSpec((1, CMP_LEN, HEAD_DIM), lambda c, b, g: (c, 0, 0)),
            pl.BlockSpec((1, CMP_LEN, HEAD_DIM, HEAD_DIM), lambda c, b, g: (c, 0, 0, 0)),
            pl.BlockSpec((1, HEAD_DIM, HEAD_DIM), lambda c, b, g: (c, 0, 0)),
        ],
        out_specs=pl.BlockSpec((1, 1, 1, n, HEAD_DIM), lambda c, b, g: (c, b, g, 0, 0)),
        compiler_params=_params(("parallel", "parallel", "parallel")),
        name="compress",
    )(P, pos, w1, w2)


def _masked_softmax(s, mask):
    s = jnp.where(mask, s, NEG_INF)
    e = jnp.where(mask, jnp.exp(s - jnp.max(s, axis=-1, keepdims=True)), 0.0)
    return e / jnp.maximum(jnp.sum(e, axis=-1, keepdims=True), 1e-30)


def _nsa_select_kernel(slopes_ref, q_ref, kc_ref, vc_ref, agg_ref, grp_ref, oc_ref, sel_ref, act_ref):
    g = pl.program_id(1)
    blk = pl.program_id(2)
    Q = Q_BLOCK
    n = kc_ref.shape[3]
    ns = sel_ref.shape[3]
    kc = kc_ref[0, 0, 0]
    vc = vc_ref[0, 0, 0]
    qb = q_ref[...].reshape(NSA_HPG * Q, HEAD_DIM).astype(BF16)
    s = _dot_nt(qb, kc) * (HEAD_DIM ** -0.5)

    t = blk * Q + lax.broadcasted_iota(I32, (Q, 1), 0)
    cid = lax.broadcasted_iota(I32, (1, n), 1)
    dist = t - (cid * CMP_STRIDE + CMP_LEN - 1)
    mask = (dist >= 0) & (cid < n - 1)
    distf = dist.astype(F32)

    psum = jnp.zeros((Q, n), F32)
    for hh in range(NSA_HPG):
        p = _masked_softmax(s[hh * Q:(hh + 1) * Q] - slopes_ref[g * NSA_HPG + hh] * distf, mask)
        oc_ref[:, hh * HEAD_DIM:(hh + 1) * HEAD_DIM] = _dot(p.astype(BF16), vc)
        psum = psum + p

    hi = psum.astype(BF16)
    lo = (psum - hi.astype(F32)).astype(BF16)
    imp = _dot(hi, agg_ref[...]) + _dot(lo, agg_ref[...])

    sid = lax.broadcasted_iota(I32, (1, ns), 1)
    sidf = sid.astype(F32)
    cur = t // SEL_LEN
    forced = (sid == 0) | (sid == cur) | (sid == cur - 1)
    valid = sid * SEL_LEN <= t
    work = jnp.where(forced, SEL_FORCE, jnp.where(valid, imp, NEG_INF))
    picked = jnp.zeros((Q, ns), F32)
    for _ in range(min(SEL_COUNT, ns)):
        m = jnp.max(work, axis=-1, keepdims=True)
        first = jnp.min(jnp.where(work == m, sidf, float(ns)), axis=-1, keepdims=True)
        hit = sidf == first
        picked = jnp.where(hit, 1.0, picked)
        work = jnp.where(hit, PICKED, work)
    sel = jnp.where(valid, picked, 0.0)
    sel_ref[0, 0] = sel.astype(sel_ref.dtype)
    anyq = jnp.max(sel, axis=0, keepdims=True)
    tiles = _dot(jnp.broadcast_to(anyq, (8, ns)).astype(BF16), grp_ref[...])
    act_ref[0, 0, 0] = (tiles > 0.5).astype(I32)


def _cmp_to_sel_matrix(n_cmp_padded, n_sel):
    r = SEL_LEN // CMP_STRIDE
    c = CMP_LEN // CMP_STRIDE
    off = np.arange(n_cmp_padded)[:, None] - r * np.arange(n_sel)[None, :] + (c - 1)
    cnt = np.minimum(c - 1, off) - np.maximum(0, off - r + 1) + 1
    return np.clip(cnt, 0, None).astype(np.float32)


def _alibi_slopes(n_heads):
    return (2.0 ** (-8.0 * (np.arange(n_heads) + 1) / n_heads)).astype(np.float32)


def _nsa_select(P, kvc, B, S):
    G, Q = NSA_KV_GROUPS, Q_BLOCK
    nq = S // Q
    n = S // CMP_STRIDE
    ns = S // SEL_LEN
    T = B * S
    agg = jnp.asarray(_cmp_to_sel_matrix(n, ns), BF16)
    per_tile = KEY_TILE // SEL_LEN
    grp = jnp.asarray((np.arange(ns)[:, None] // per_tile) == np.arange(LANE)[None, :], BF16)
    slopes = jnp.asarray(_alibi_slopes(NSA_HEADS))
    grid_spec = pltpu.PrefetchScalarGridSpec(
        num_scalar_prefetch=1,
        grid=(B, G, nq),
        in_specs=[
            pl.BlockSpec((NSA_HPG, Q, LANE), lambda b, g, i, s_: (SLAB_NSA_Q // NSA_HPG + g, b * nq + i, 0)),
            pl.BlockSpec((1, 1, 1, n, HEAD_DIM), lambda b, g, i, s_: (0, b, g, 0, 0)),
            pl.BlockSpec((1, 1, 1, n, HEAD_DIM), lambda b, g, i, s_: (1, b, g, 0, 0)),
            pl.BlockSpec((n, ns), lambda b, g, i, s_: (0, 0)),
            pl.BlockSpec((ns, LANE), lambda b, g, i, s_: (0, 0)),
        ],
        out_specs=[
            pl.BlockSpec((Q, NSA_HPG * HEAD_DIM), lambda b, g, i, s_: (b * nq + i, g)),
            pl.BlockSpec((1, 1, Q, ns), lambda b, g, i, s_: (b, g, i, 0)),
            pl.BlockSpec((1, 1, 1, 8, LANE), lambda b, g, i, s_: (b, g, i, 0, 0)),
        ],
    )
    return pl.pallas_call(
        _nsa_select_kernel,
        out_shape=[
            jax.ShapeDtypeStruct((T, NSA_HEADS * HEAD_DIM), F32),
            jax.ShapeDtypeStruct((B, G, S, ns), BF16),
            jax.ShapeDtypeStruct((B, G, nq, 8, LANE), I32),
        ],
        grid_spec=grid_spec,
        compiler_params=_params(("parallel", "parallel", "parallel")),
        name="nsa_select",
    )(slopes, P, kvc, kvc, agg, grp)


def _nsa_attend_kernel(slopes_ref, act_ref, q_ref, ks_ref, vs_ref, kw_ref, vw_ref, sel_ref, oc_ref,
                       sm_ref, ng_ref, o_ref, m_scr, l_scr, acc_scr, *, tiles_max):
    b = pl.program_id(0)
    g = pl.program_id(1)
    blk = pl.program_id(2)
    nq = pl.num_programs(2)
    Q, H = Q_BLOCK, NSA_HPG
    ns = sel_ref.shape[3]
    per_tile = KEY_TILE // SEL_LEN
    scale = HEAD_DIM ** -0.5

    qb = q_ref[...].reshape(H * Q, HEAD_DIM).astype(BF16)
    t = blk * Q + lax.broadcasted_iota(I32, (Q, 1), 0)
    selb = sel_ref[0, 0]
    slopes = [slopes_ref[g * H + hh] for hh in range(H)]

    m_scr[...] = jnp.full_like(m_scr, NEG_INF)
    l_scr[...] = jnp.zeros_like(l_scr)
    acc_scr[...] = jnp.zeros_like(acc_scr)
    act_base = ((b * pl.num_programs(1) + g) * nq + blk) * tiles_max

    def tile_body(j, carry):
        @pl.when(act_ref[act_base + j] > 0)
        def _():
            start = pl.multiple_of(j * KEY_TILE, KEY_TILE)
            kt = ks_ref[0, pl.ds(start, KEY_TILE), :]
            vt = vs_ref[0, pl.ds(start, KEY_TILE), :]
            s = _dot_nt(qb, kt) * scale
            blk_of_key = j * per_tile + lax.broadcasted_iota(I32, (1, KEY_TILE), 1) // SEL_LEN
            expand = (lax.broadcasted_iota(I32, (ns, 1), 0) == blk_of_key).astype(BF16)
            dist = t - (start + lax.broadcasted_iota(I32, (1, KEY_TILE), 1))
            mask = (_dot(selb, expand) > 0.5) & (dist >= 0)
            distf = dist.astype(F32)
            for hh in range(H):
                rows = slice(hh * Q, (hh + 1) * Q)
                sh = jnp.where(mask, s[rows] - slopes[hh] * distf, NEG_INF)
                m_old = m_scr[rows]
                m_new = jnp.maximum(m_old, jnp.max(sh, axis=-1, keepdims=True))
                alpha = jnp.exp(m_old - m_new)
                p = jnp.where(mask, jnp.exp(sh - m_new), 0.0)
                l_scr[rows] = alpha * l_scr[rows] + jnp.sum(p, axis=-1, keepdims=True)
                acc_scr[rows] = alpha * acc_scr[rows] + _dot(p.astype(BF16), vt)
                m_scr[rows] = m_new
        return carry

    lax.fori_loop(0, (blk * Q + Q - 1) // KEY_TILE + 1, tile_body, 0)

    span = WINDOW + Q
    base = pl.multiple_of(jnp.maximum(blk * Q - WINDOW, 0), Q)
    kw = kw_ref[0, pl.ds(base, span), :]
    vw = vw_ref[0, pl.ds(base, span), :]
    sw = _dot_nt(qb, kw) * scale
    dist_w = t - (base + lax.broadcasted_iota(I32, (1, span), 1))
    mask_w = (dist_w >= 0) & (dist_w < WINDOW)
    distf_w = dist_w.astype(F32)

    sig = _sigmoid(sm_ref[0])
    lane = lax.broadcasted_iota(I32, (1, LANE), 1)
    for hh in range(H):
        rows = slice(hh * Q, (hh + 1) * Q)
        cols = slice(hh * HEAD_DIM, (hh + 1) * HEAD_DIM)
        pw = _masked_softmax(sw[rows] - slopes[hh] * distf_w, mask_w)
        o_w = _dot(pw.astype(BF16), vw)
        o_s = acc_scr[rows] / jnp.maximum(l_scr[rows], 1e-30)
        gl = LANE_GATE + (g * H + hh) * 3
        o = (_pick_lane(sig, lane, gl) * oc_ref[:, cols] + _pick_lane(sig, lane, gl + 1) * o_s
             + _pick_lane(sig, lane, gl + 2) * o_w)
        on = o * lax.rsqrt(jnp.mean(o * o, axis=-1, keepdims=True) + NORM_EPS) * ng_ref[...]
        o_ref[:, cols] = on.astype(o_ref.dtype)


def _nsa_attend(P, kv_bf16, sel, act_flat, o_c, B, S, norm_g):
    G, Q = NSA_KV_GROUPS, Q_BLOCK
    nq = S // Q
    ns = S // SEL_LEN
    T = B * S
    slopes = jnp.asarray(_alibi_slopes(NSA_HEADS))

    def kv(base):
        return pl.BlockSpec((1, S, LANE), lambda b, g, i, *_: (base + g, b, 0))

    grid_spec = pltpu.PrefetchScalarGridSpec(
        num_scalar_prefetch=2,
        grid=(B, G, nq),
        in_specs=[
            pl.BlockSpec((NSA_HPG, Q, LANE), lambda b, g, i, *_: (SLAB_NSA_Q // NSA_HPG + g, b * nq + i, 0)),
            kv(0), kv(2), kv(4), kv(6),
            pl.BlockSpec((1, 1, Q, ns), lambda b, g, i, *_: (b, g, i, 0)),
            pl.BlockSpec((Q, NSA_HPG * HEAD_DIM), lambda b, g, i, *_: (b * nq + i, g)),
            pl.BlockSpec((1, Q, LANE), lambda b, g, i, *_: (SLAB_SMALL, b * nq + i, 0)),
            pl.BlockSpec((1, LANE), lambda b, g, i, *_: (0, 0)),
        ],
        out_specs=pl.BlockSpec((Q, NSA_HPG * HEAD_DIM), lambda b, g, i, *_: (b * nq + i, g)),
        scratch_shapes=[
            pltpu.VMEM((NSA_HPG * Q, 1), F32), pltpu.VMEM((NSA_HPG * Q, 1), F32),
            pltpu.VMEM((NSA_HPG * Q, HEAD_DIM), F32),
        ],
    )
    return pl.pallas_call(
        functools.partial(_nsa_attend_kernel, tiles_max=S // KEY_TILE),
        out_shape=jax.ShapeDtypeStruct((T, NSA_HEADS * HEAD_DIM), BF16),
        grid_spec=grid_spec,
        compiler_params=_params(("parallel", "parallel", "arbitrary")),
        name="nsa_attend",
    )(slopes, act_flat, P, kv_bf16, kv_bf16, kv_bf16, kv_bf16, sel, o_c, P, norm_g.reshape(1, LANE))


def _out_proj_kernel(ydn_ref, ynsa_ref, x_ref, wo_ref, g_ref, wrh_ref, wrl_ref, br_ref,
                     x1_ref, h2_ref, ri_ref, rw_ref, cnt_ref, carry_scr):
    tm = x_ref.shape[0]
    half = ydn_ref.shape[1]

    @pl.when(pl.program_id(0) == 0)
    def _():
        carry_scr[...] = jnp.zeros_like(carry_scr)

    x1 = x_ref[...] + _dot(ydn_ref[...], wo_ref[0:half, :]) + _dot(ynsa_ref[...], wo_ref[half:, :])
    x1_ref[...] = x1
    h2 = x1 * lax.rsqrt(jnp.mean(x1 * x1, axis=-1, keepdims=True) + NORM_EPS) * g_ref[...]
    h2_ref[...] = h2

    hi = h2.astype(BF16)
    lo = (h2 - hi.astype(F32)).astype(BF16)
    lg = _dot(hi, wrh_ref[...]) + _dot(hi, wrl_ref[...]) + _dot(lo, wrh_ref[...])
    lane = lax.broadcasted_iota(I32, (1, LANE), 1)
    lanef = lane.astype(F32)
    biased = lg + br_ref[...]

    is_grp = lane < N_GROUPS
    gmax = jnp.max(jnp.where(is_grp, lg, NEG_INF), axis=-1, keepdims=True)
    eg = jnp.where(is_grp, jnp.exp(lg - gmax), 0.0)
    pg = eg / jnp.sum(eg, axis=-1, keepdims=True)
    gb = jnp.where(is_grp, biased, NEG_INF)
    g_sel = jnp.min(jnp.where(gb == jnp.max(gb, axis=-1, keepdims=True), lanef, float(LANE)),
                    axis=-1, keepdims=True)
    p_sel = _pick_lane(pg, lanef, g_sel)

    lo_lane = N_GROUPS + g_sel * EXPERTS_PER_GROUP
    in_grp = (lanef >= lo_lane) & (lanef < lo_lane + EXPERTS_PER_GROUP)
    v1 = jnp.where(in_grp, biased, NEG_INF)
    i1 = jnp.min(jnp.where(v1 == jnp.max(v1, axis=-1, keepdims=True), lanef, float(LANE)),
                 axis=-1, keepdims=True)
    v2 = jnp.where(lanef == i1, NEG_INF * 2, v1)
    i2 = jnp.min(jnp.where(v2 == jnp.max(v2, axis=-1, keepdims=True), lanef, float(LANE)),
                 axis=-1, keepdims=True)
    el1 = _pick_lane(lg, lanef, i1)
    el2 = _pick_lane(lg, lanef, i2)
    mx = jnp.maximum(el1, el2)
    e1 = jnp.exp(el1 - mx)
    e2 = jnp.exp(el2 - mx)
    rw_ref[...] = jnp.where(lane == 0, p_sel * (e1 / (e1 + e2)),
                            jnp.where(lane == 1, p_sel * (e2 / (e1 + e2)), 0.0))

    ex1 = i1 - N_GROUPS
    ex2 = i2 - N_GROUPS
    oh1 = lanef == ex1
    oh2 = lanef == ex2
    both = jnp.where(oh1 | oh2, 1.0, 0.0)
    ri = lax.broadcasted_iota(I32, (tm, tm), 0)
    ci = lax.broadcasted_iota(I32, (tm, tm), 1)
    before = _dot((ci < ri).astype(BF16), both.astype(BF16)) + carry_scr[...]
    r1 = jnp.sum(jnp.where(oh1, before, 0.0), axis=-1, keepdims=True)
    r2 = jnp.sum(jnp.where(oh2, before, 0.0), axis=-1, keepdims=True)
    packed = jnp.where(lane == 0, ex1, jnp.where(lane == 1, ex2,
                       jnp.where(lane == 2, r1, jnp.where(lane == 3, r2, 0.0))))
    ri_ref[...] = packed.astype(I32)
    total = carry_scr[...] + jnp.sum(both, axis=0, keepdims=True)
    carry_scr[...] = total
    cnt_ref[...] = jnp.broadcast_to(total, cnt_ref.shape).astype(I32)


def _out_proj(ydn, ynsa, x2d, wo, g, wr_hi, wr_lo, br, tm=256):
    T, D = x2d.shape
    half = ydn.shape[1]
    row = lambda i: (i, 0)
    fixed = lambda i: (0, 0)
    return pl.pallas_call(
        _out_proj_kernel,
        out_shape=[
            jax.ShapeDtypeStruct((T, D), F32), jax.ShapeDtypeStruct((T, D), F32),
            jax.ShapeDtypeStruct((T, LANE), I32), jax.ShapeDtypeStruct((T, LANE), F32),
            jax.ShapeDtypeStruct((8, LANE), I32),
        ],
        grid=(T // tm,),
        in_specs=[
            pl.BlockSpec((tm, half), row), pl.BlockSpec((tm, half), row), pl.BlockSpec((tm, D), row),
            pl.BlockSpec((D, D), fixed), pl.BlockSpec((1, D), fixed),
            pl.BlockSpec((D, LANE), fixed), pl.BlockSpec((D, LANE), fixed), pl.BlockSpec((1, LANE), fixed),
        ],
        out_specs=[
            pl.BlockSpec((tm, D), row), pl.BlockSpec((tm, D), row),
            pl.BlockSpec((tm, LANE), row), pl.BlockSpec((tm, LANE), row), pl.BlockSpec((8, LANE), fixed),
        ],
        scratch_shapes=[pltpu.VMEM((1, LANE), F32)],
        compiler_params=_params(("arbitrary",)),
        name="out_proj_router",
    )(ydn, ynsa, x2d, wo, g, wr_hi, wr_lo, br)


def _row_copy(src_hbm, row, dst, slot, sem):
    return pltpu.make_async_copy(src_hbm.at[pl.ds(row, 1)], dst.at[pl.ds(slot, 1)], sem)


def _moe_experts_kernel(be_ref, nu_ref, rt_ref, h_hbm, wg_ref, wu_ref, wd_ref, y_ref,
                        xs_scr, wg_scr, wu_scr, wd_scr, sem):
    i = pl.program_id(0)
    R = xs_scr.shape[0]

    @pl.when(i < nu_ref[0])
    def _():
        def issue(r, c):
            _row_copy(h_hbm, rt_ref[i * R + r], xs_scr, r, sem).start()
            return c
        lax.fori_loop(0, R, issue, 0)

        @pl.when((i == 0) | (be_ref[jnp.maximum(i - 1, 0)] != be_ref[i]))
        def _():
            wg_scr[...] = wg_ref[0].astype(BF16)
            wu_scr[...] = wu_ref[0].astype(BF16)
            wd_scr[...] = wd_ref[0].astype(BF16)

        def wait(r, c):
            _row_copy(h_hbm, 0, xs_scr, r, sem).wait()
            return c
        lax.fori_loop(0, R, wait, 0)

        xb = xs_scr[...].astype(BF16)
        hid = _silu(_dot(xb, wg_scr[...])) * _dot(xb, wu_scr[...])
        y_ref[...] = _dot(hid.astype(BF16), wd_scr[...])

    @pl.when(i >= nu_ref[0])
    def _():
        y_ref[...] = jnp.zeros_like(y_ref)


def _moe_experts(h2, block_expert, n_used, row_tok, w_gate, w_up, w_down):
    T, D = h2.shape
    R = MOE_ROWS
    n_blocks = block_expert.shape[0]
    DE = w_gate.shape[2]
    grid_spec = pltpu.PrefetchScalarGridSpec(
        num_scalar_prefetch=3,
        grid=(n_blocks,),
        in_specs=[
            pl.BlockSpec(memory_space=pl.ANY),
            pl.BlockSpec((1, D, DE), lambda i, be, nu, rt: (be[i], 0, 0)),
            pl.BlockSpec((1, D, DE), lambda i, be, nu, rt: (be[i], 0, 0)),
            pl.BlockSpec((1, DE, D), lambda i, be, nu, rt: (be[i], 0, 0)),
        ],
        out_specs=pl.BlockSpec((R, D), lambda i, be, nu, rt: (i, 0)),
        scratch_shapes=[
            pltpu.VMEM((R, D), F32), pltpu.VMEM((D, DE), BF16), pltpu.VMEM((D, DE), BF16),
            pltpu.VMEM((DE, D), BF16), pltpu.SemaphoreType.DMA,
        ],
    )
    return pl.pallas_call(
        _moe_experts_kernel,
        out_shape=jax.ShapeDtypeStruct((n_blocks * R, D), F32),
        grid_spec=grid_spec,
        compiler_params=_params(("arbitrary",)),
        name="moe_experts",
    )(block_expert, n_used, row_tok, h2, w_gate, w_up, w_down)


def _moe_combine_kernel(dest_ref, x1_ref, rw_ref, g_ref, ys_hbm, o_ref, buf, sem):
    i = pl.program_id(0)
    tm = x1_ref.shape[0]

    def issue(r, c):
        for k in range(2):
            _row_copy(ys_hbm, dest_ref[(i * tm + r) * 2 + k], buf.at[k], r, sem).start()
        return c
    lax.fori_loop(0, tm, issue, 0)

    def wait(r, c):
        for k in range(2):
            _row_copy(ys_hbm, 0, buf.at[k], r, sem).wait()
        return c
    lax.fori_loop(0, tm, wait, 0)

    rw = rw_ref[...]
    x2 = x1_ref[...] + buf[0] * rw[:, 0:1] + buf[1] * rw[:, 1:2]
    o_ref[...] = x2 * lax.rsqrt(jnp.mean(x2 * x2, axis=-1, keepdims=True) + NORM_EPS) * g_ref[...]


def _moe_combine(dest_flat, x1, rw, g, ys, tm=128):
    T, D = x1.shape
    grid_spec = pltpu.PrefetchScalarGridSpec(
        num_scalar_prefetch=1,
        grid=(T // tm,),
        in_specs=[
            pl.BlockSpec((tm, D), lambda i, d: (i, 0)),
            pl.BlockSpec((tm, LANE), lambda i, d: (i, 0)),
            pl.BlockSpec((1, D), lambda i, d: (0, 0)),
            pl.BlockSpec(memory_space=pl.ANY),
        ],
        out_specs=pl.BlockSpec((tm, D), lambda i, d: (i, 0)),
        scratch_shapes=[pltpu.VMEM((2, tm, D), F32), pltpu.SemaphoreType.DMA],
    )
    return pl.pallas_call(
        _moe_combine_kernel,
        out_shape=jax.ShapeDtypeStruct((T, D), F32),
        grid_spec=grid_spec,
        compiler_params=_params(("arbitrary",)),
        name="moe_combine",
    )(dest_flat, x1, rw, g, ys)


def _slab_weights(w_in):
    D = w_in.shape[0]
    n_dn = 4 * DN_HEADS * HEAD_DIM
    n_small_a = 2 * DN_HEADS
    n_nsa = NSA_HEADS * HEAD_DIM + 6 * NSA_KV_GROUPS * HEAD_DIM
    c0 = n_dn + n_small_a
    c1 = c0 + n_nsa
    n_gate = 3 * NSA_HEADS
    pad = N_SLABS * LANE - (n_dn + n_nsa + n_small_a + n_gate)
    return jnp.concatenate(
        [w_in[:, :n_dn], w_in[:, c0:c1], w_in[:, n_dn:c0], w_in[:, c1:c1 + n_gate],
         jnp.zeros((D, pad), w_in.dtype)], axis=1).astype(BF16)


def _layer(x, norm_mix_g, w_in, conv_w, dt_bias, a_log, dn_norm_g, cmp_pos_k, cmp_w1_k, cmp_w2_k,
           cmp_pos_v, cmp_w1_v, cmp_w2_v, nsa_norm_g, w_out, norm_ffn_g, w_group, b_group,
           w_router, b_router, w_gate, w_up, w_down, norm_final_g):
    B, S, D = x.shape
    T = B * S
    x2d = x.reshape(T, D)

    P = _in_proj(x2d, norm_mix_g.reshape(1, D), _slab_weights(w_in))
    ydn = _deltanet(P, B, S, conv_w, dt_bias, a_log, dn_norm_g)

    kvc = _compress(P, B, S, jnp.stack([cmp_pos_k, cmp_pos_v]), jnp.stack([cmp_w1_k, cmp_w1_v]),
                    jnp.stack([cmp_w2_k, cmp_w2_v]))
    o_c, sel, act = _nsa_select(P, kvc, B, S)
    tiles_max = S // KEY_TILE
    act_flat = act[:, :, :, 0, :tiles_max].reshape(-1)
    kv_bf16 = P[SLAB_K_SEL:SLAB_K_SEL + 8].astype(BF16)
    ynsa = _nsa_attend(P, kv_bf16, sel, act_flat, o_c, B, S, nsa_norm_g)

    wr = jnp.zeros((D, LANE), F32).at[:, :N_GROUPS].set(w_group).at[:, N_GROUPS:N_GROUPS + N_EXPERTS].set(w_router)
    wr_hi = wr.astype(BF16)
    wr_lo = (wr - wr_hi.astype(F32)).astype(BF16)
    br = jnp.zeros((1, LANE), F32).at[0, :N_GROUPS].set(b_group).at[0, N_GROUPS:N_GROUPS + N_EXPERTS].set(b_router)
    x1, h2, ri, rw, cnt = _out_proj(ydn, ynsa, x2d, w_out.astype(BF16), norm_ffn_g.reshape(1, D),
                                    wr_hi, wr_lo, br)

    R = MOE_ROWS
    counts = cnt[0, :N_EXPERTS]
    padded = (counts + R - 1) // R * R
    pends = jnp.cumsum(padded)
    pstarts = pends - padded
    dest = pstarts[ri[:, 0:2]] + ri[:, 2:4]
    n_blocks = (2 * T + R - 1) // R + N_EXPERTS
    row_tok = jnp.zeros((n_blocks * R,), I32).at[dest.reshape(-1)].set(jnp.repeat(jnp.arange(T, dtype=I32), 2))
    block_start = jnp.arange(n_blocks, dtype=I32) * R
    block_expert = jnp.minimum(jnp.sum(block_start[:, None] >= pends[None, :], axis=1),
                               N_EXPERTS - 1).astype(I32)
    n_used = (pends[-1:] // R).astype(I32)

    ys = _moe_experts(h2, block_expert, n_used, row_tok, w_gate, w_up, w_down)
    out = _moe_combine(dest.reshape(-1).astype(I32), x1, rw, norm_final_g.reshape(1, D), ys)
    return out.reshape(B, S, D)


def kernel(x, norm_mix_g, w_in, conv_w, dt_bias, a_log, dn_norm_g, cmp_pos_k, cmp_w1_k, cmp_w2_k,
           cmp_pos_v, cmp_w1_v, cmp_w2_v, nsa_norm_g, w_out, norm_ffn_g, w_group, b_group,
           w_router, b_router, w_gate, w_up, w_down, norm_final_g):
    assert w_in.shape[0] == 1, "one layer"
    return _layer(x, norm_mix_g[0], w_in[0], conv_w[0], dt_bias[0], a_log[0], dn_norm_g[0],
                  cmp_pos_k[0], cmp_w1_k[0], cmp_w2_k[0], cmp_pos_v[0], cmp_w1_v[0], cmp_w2_v[0],
                  nsa_norm_g[0], w_out[0], norm_ffn_g[0], w_group[0], b_group[0], w_router[0],
                  b_router[0], w_gate[0], w_up[0], w_down[0], norm_final_g)
```

```python
import functools
import math

import numpy as np
import jax
import jax.numpy as jnp
from jax import lax
from jax.experimental import pallas as pl
from jax.experimental.pallas import tpu as pltpu

F32 = jnp.float32
BF16 = jnp.bfloat16
I32 = jnp.int32

LANE = 128
HEAD_DIM = 128
DN_HEADS = 8
NSA_HEADS = 8
NSA_KV_GROUPS = 2
NSA_HPG = NSA_HEADS // NSA_KV_GROUPS
CONV_WIDTH = 4
DN_CHUNK = 64
CMP_LEN = 32
CMP_STRIDE = 16
SEL_LEN = 64
SEL_COUNT = 16
WINDOW = 512
Q_BLOCK = 128
N_GROUPS = 8
EXPERTS_PER_GROUP = 8
N_EXPERTS = N_GROUPS * EXPERTS_PER_GROUP
NORM_EPS = 1e-6
NEG_INF = -1e30
SEL_FORCE = 1e9
PICKED = -3e38

SLAB_DN_Q, SLAB_DN_K, SLAB_DN_V, SLAB_DN_Z = 0, 8, 16, 24
SLAB_NSA_Q = 32
SLAB_K_CMP, SLAB_V_CMP, SLAB_K_SEL, SLAB_V_SEL, SLAB_K_WIN, SLAB_V_WIN = 40, 42, 44, 46, 48, 50
SLAB_SMALL = 52
N_SLABS = 54
LANE_DN_B, LANE_DN_A, LANE_GATE = 0, 8, 16

KEY_TILE = 256
MOE_ROWS = 256

VMEM_LIMIT = 56 * 1024 * 1024


def _params(sem, vmem=VMEM_LIMIT):
    return pltpu.CompilerParams(dimension_semantics=sem, vmem_limit_bytes=vmem)


def _dot(a, b):
    return jnp.dot(a, b, preferred_element_type=F32)


def _dot_nt(a, b):
    return lax.dot_general(a, b, (((1,), (1,)), ((), ())), preferred_element_type=F32)


def _dot_exact(a, b):
    return jnp.dot(a, b, preferred_element_type=F32, precision=lax.Precision.HIGHEST)


def _sigmoid(x):
    return 1.0 / (1.0 + jnp.exp(-x))


def _silu(x):
    return x * _sigmoid(x)


def _pick_lane(x, lane_ids, idx):
    return jnp.sum(jnp.where(lane_ids == idx, x, 0.0), axis=1, keepdims=True)


def _in_proj_kernel(x_ref, g_ref, w_ref, o_ref, h_scr):
    @pl.when(pl.program_id(1) == 0)
    def _():
        xf = x_ref[...]
        ms = jnp.mean(xf * xf, axis=-1, keepdims=True)
        h_scr[...] = (xf * lax.rsqrt(ms + NORM_EPS) * g_ref[...]).astype(BF16)

    acc = _dot(h_scr[...], w_ref[...])
    for k in range(o_ref.shape[0]):
        o_ref[k] = acc[:, k * LANE:(k + 1) * LANE]


def _in_proj(x2d, g, w_slabs, tm=512, tn=768):
    T, D = x2d.shape
    NP = w_slabs.shape[1]
    ns = tn // LANE
    return pl.pallas_call(
        _in_proj_kernel,
        out_shape=jax.ShapeDtypeStruct((NP // LANE, T, LANE), F32),
        grid=(T // tm, NP // tn),
        in_specs=[
            pl.BlockSpec((tm, D), lambda i, j: (i, 0)),
            pl.BlockSpec((1, D), lambda i, j: (0, 0)),
            pl.BlockSpec((D, tn), lambda i, j: (0, j)),
        ],
        out_specs=pl.BlockSpec((ns, tm, LANE), lambda i, j: (j, i, 0)),
        scratch_shapes=[pltpu.VMEM((tm, D), BF16)],
        compiler_params=_params(("parallel", "arbitrary")),
        name="in_proj",
    )(x2d, g, w_slabs)


DN_HEADS_PER_ITER = 4


def _deltanet_kernel(p_ref, sm_ref, cw_ref, alog_ref, dtb_ref, ng_ref, o_ref,
                     ext_scr, s_scr, gc_scr, gl_scr, gct_scr, beta_scr):
    L = p_ref.shape[1]
    C = DN_CHUNK
    H = DN_HEADS

    @pl.when(pl.program_id(1) == 0)
    def _():
        ext_scr[:, 0:8, :] = jnp.zeros((3 * H, 8, LANE), F32)
        s_scr[...] = jnp.zeros_like(s_scr)

    sm = sm_ref[0]
    lane = lax.broadcasted_iota(I32, (1, LANE), 1)
    xa = sm + dtb_ref[...]
    softplus = jnp.maximum(xa, 0.0) + jnp.log1p(jnp.exp(-jnp.abs(xa)))
    ld = -jnp.exp(alog_ref[...]) * softplus
    ri = lax.broadcasted_iota(I32, (L, L), 0)
    ci = lax.broadcasted_iota(I32, (L, L), 1)
    same = (ri // C) == (ci // C)
    causal = same & (ci <= ri)
    gc_slab = _dot_exact(jnp.where(causal, 1.0, 0.0), ld)
    gc_scr[...] = gc_slab
    gct_scr[...] = gc_slab.T
    gl_scr[...] = _dot_exact(jnp.where(same, 1.0, 0.0), ld)
    beta_scr[...] = _sigmoid(sm)

    def conv_silu(slab, h):
        idx = slab + h
        ext_scr[idx, 8:, :] = p_ref[idx]
        w = cw_ref[idx]
        acc = ext_scr[idx, pl.ds(8 - (CONV_WIDTH - 1), L), :] * w[0:1]
        for j in range(1, CONV_WIDTH):
            acc = acc + ext_scr[idx, pl.ds(8 - (CONV_WIDTH - 1) + j, L), :] * w[j:j + 1]
        ext_scr[idx, 0:8, :] = ext_scr[idx, L:L + 8, :]
        return _silu(acc)

    def l2n(x):
        return x * lax.rsqrt(jnp.sum(x * x, axis=-1, keepdims=True) + NORM_EPS)

    def head_group(hg, carry):
        hs = [hg * DN_HEADS_PER_ITER + dh for dh in range(DN_HEADS_PER_ITER)]
        n = range(len(hs))
        qs = [l2n(conv_silu(SLAB_DN_Q, h)) * (HEAD_DIM ** -0.5) for h in hs]
        ks = [l2n(conv_silu(SLAB_DN_K, h)) for h in hs]
        vs = [conv_silu(SLAB_DN_V, h) for h in hs]
        betas = [_pick_lane(beta_scr[...], lane, LANE_DN_B + h) for h in hs]
        gcols = [_pick_lane(gc_scr[...], lane, LANE_DN_A + h) for h in hs]
        glasts = [_pick_lane(gl_scr[...], lane, LANE_DN_A + h) for h in hs]
        grows = [gct_scr[pl.ds(LANE_DN_A + h, 1), :] for h in hs]
        states = [s_scr[h] for h in hs]
        zs = [p_ref[SLAB_DN_Z + h] for h in hs]

        decays = [jnp.where(causal, jnp.exp(jnp.where(causal, gcols[i] - grows[i], 0.0)), 0.0) for i in n]
        egcs = [jnp.exp(gcols[i]) for i in n]
        kbs = [ks[i] * betas[i] for i in n]
        kts = [ks[i].astype(BF16) for i in n]
        lms = [jnp.where(ci < ri, _dot_nt(kbs[i].astype(BF16), kts[i]) * decays[i], 0.0) for i in n]
        tinvs = [jnp.where(ri == ci, 1.0, 0.0) - lms[i] for i in n]
        pws = lms
        for _ in range(int(math.log2(C)) - 1):
            pwbs = [pws[i].astype(BF16) for i in n]
            pws = [_dot(pwbs[i], pwbs[i]) for i in n]
            tinvs = [tinvs[i] + _dot(tinvs[i].astype(BF16), pws[i].astype(BF16)) for i in n]
        tbs = [tinvs[i].astype(BF16) for i in n]
        us = [_dot(tbs[i], (vs[i] * betas[i]).astype(BF16)) for i in n]
        ws = [_dot(tbs[i], (kbs[i] * egcs[i]).astype(BF16)) for i in n]
        qks = [_dot_nt(qs[i].astype(BF16), kts[i]) * decays[i] for i in n]
        q_decs = [(qs[i] * egcs[i]).astype(BF16) for i in n]
        k_dec_ts = [(ks[i] * jnp.exp(glasts[i] - gcols[i])).T for i in n]

        outs = [[] for _ in n]
        for c in range(L // C):
            sl = slice(c * C, (c + 1) * C)
            sbs = [states[i].astype(BF16) for i in n]
            vnbs = [(us[i][sl] - _dot(ws[i][sl].astype(BF16), sbs[i])).astype(BF16) for i in n]
            for i in n:
                outs[i].append(_dot(q_decs[i][sl], sbs[i]) + _dot(qks[i][sl, sl].astype(BF16), vnbs[i]))
            states = [states[i] * jnp.exp(glasts[i][c * C:c * C + 1, :])
                      + _dot(k_dec_ts[i][:, sl].astype(BF16), vnbs[i]) for i in n]

        for i, h in enumerate(hs):
            s_scr[h] = states[i]
            o = jnp.concatenate(outs[i], axis=0)
            on = o * lax.rsqrt(jnp.mean(o * o, axis=-1, keepdims=True) + NORM_EPS) * ng_ref[...]
            o_ref[h] = (on * _silu(zs[i])).astype(o_ref.dtype)
        return carry

    lax.fori_loop(0, H // DN_HEADS_PER_ITER, head_group, 0)


def _deltanet(P, B, S, conv_w, dt_bias, a_log, norm_g, L=256):
    T = B * S
    nL = S // L
    H = DN_HEADS
    cw = conv_w.reshape(CONV_WIDTH, 3 * H, LANE).transpose(1, 0, 2)
    alog = jnp.zeros((1, LANE), F32).at[0, LANE_DN_A:LANE_DN_A + H].set(a_log)
    dtb = jnp.zeros((1, LANE), F32).at[0, LANE_DN_A:LANE_DN_A + H].set(dt_bias)
    vec = pl.BlockSpec((1, LANE), lambda b, i: (0, 0))
    return pl.pallas_call(
        _deltanet_kernel,
        out_shape=jax.ShapeDtypeStruct((H, T, LANE), BF16),
        grid=(B, nL),
        in_specs=[
            pl.BlockSpec((4 * H, L, LANE), lambda b, i: (0, b * nL + i, 0)),
            pl.BlockSpec((1, L, LANE), lambda b, i: (SLAB_SMALL, b * nL + i, 0)),
            pl.BlockSpec((3 * H, CONV_WIDTH, LANE), lambda b, i: (0, 0, 0)),
            vec, vec, vec,
        ],
        out_specs=pl.BlockSpec((H, L, LANE), lambda b, i: (0, b * nL + i, 0)),
        scratch_shapes=[
            pltpu.VMEM((3 * H, L + 8, LANE), F32), pltpu.VMEM((H, HEAD_DIM, HEAD_DIM), F32),
            pltpu.VMEM((L, LANE), F32), pltpu.VMEM((L, LANE), F32), pltpu.VMEM((LANE, L), F32),
            pltpu.VMEM((L, LANE), F32),
        ],
        compiler_params=_params(("parallel", "arbitrary")),
        name="deltanet",
    )(P, P, cw, alog, dtb, norm_g.reshape(1, LANE))


def _gelu_tanh(x):
    return 0.5 * x * (1.0 + jnp.tanh(math.sqrt(2.0 / math.pi) * (x + 0.044715 * (x * x * x))))


def _compress_kernel(x_ref, pos_ref, w1_ref, w2_ref, o_ref):
    n = o_ref.shape[3]
    half = CMP_LEN // 2
    y1 = jnp.zeros((n, HEAD_DIM), F32)
    y2 = jnp.zeros((n, HEAD_DIM), F32)
    for l in range(half):
        xl = x_ref[0, pl.ds(l, n, stride=CMP_STRIDE), :]
        y1 = y1 + _dot((xl + pos_ref[0, l:l + 1, :]).astype(BF16), w1_ref[0, l].astype(BF16))
        y2 = y2 + _dot((xl + pos_ref[0, half + l:half + l + 1, :]).astype(BF16),
                       w1_ref[0, half + l].astype(BF16))
    hid = _gelu_tanh(y1 + pltpu.roll(y2, n - 1, 0))
    row = lax.broadcasted_iota(I32, (n, 1), 0)
    out = _dot(hid.astype(BF16), w2_ref[0].astype(BF16))
    o_ref[0, 0, 0] = jnp.where(row < n - 1, out, 0.0).astype(o_ref.dtype)


def _compress(P, B, S, pos, w1, w2):
    n = S // CMP_STRIDE
    G = NSA_KV_GROUPS
    return pl.pallas_call(
        _compress_kernel,
        out_shape=jax.ShapeDtypeStruct((2, B, G, n, HEAD_DIM), BF16),
        grid=(2, B, G),
        in_specs=[
            pl.BlockSpec((1, S, LANE), lambda c, b, g: (SLAB_K_CMP + 2 * c + g, b, 0)),
            pl.BlockSpec((1, CMP_LEN, HEAD_DIM), lambda c, b, g: (c, 0, 0)),
            pl.BlockSpec((1, CMP_LEN, HEAD_DIM, HEAD_DIM), lambda c, b, g: (c, 0, 0, 0)),
            pl.BlockSpec((1, HEAD_DIM, HEAD_DIM), lambda c, b, g: (c, 0, 0)),
        ],
        out_specs=pl.BlockSpec((1, 1, 1, n, HEAD_DIM), lambda c, b, g: (c, b, g, 0, 0)),
        compiler_params=_params(("parallel", "parallel", "parallel")),
        name="compress",
    )(P, pos, w1, w2)


def _masked_softmax(s, mask):
    s = jnp.where(mask, s, NEG_INF)
    e = jnp.where(mask, jnp.exp(s - jnp.max(s, axis=-1, keepdims=True)), 0.0)
    return e / jnp.maximum(jnp.sum(e, axis=-1, keepdims=True), 1e-30)


def _nsa_select_kernel(slopes_ref, q_ref, kc_ref, vc_ref, agg_ref, grp_ref, oc_ref, sel_ref, act_ref):
    g = pl.program_id(1)
    blk = pl.program_id(2)
    Q = Q_BLOCK
    n = kc_ref.shape[3]
    ns = sel_ref.shape[3]
    kc = kc_ref[0, 0, 0]
    vc = vc_ref[0, 0, 0]
    qb = q_ref[...].reshape(NSA_HPG * Q, HEAD_DIM).astype(BF16)
    s = _dot_nt(qb, kc) * (HEAD_DIM ** -0.5)

    t = blk * Q + lax.broadcasted_iota(I32, (Q, 1), 0)
    cid = lax.broadcasted_iota(I32, (1, n), 1)
    dist = t - (cid * CMP_STRIDE + CMP_LEN - 1)
    mask = (dist >= 0) & (cid < n - 1)
    distf = dist.astype(F32)

    psum = jnp.zeros((Q, n), F32)
    for hh in range(NSA_HPG):
        p = _masked_softmax(s[hh * Q:(hh + 1) * Q] - slopes_ref[g * NSA_HPG + hh] * distf, mask)
        oc_ref[:, hh * HEAD_DIM:(hh + 1) * HEAD_DIM] = _dot(p.astype(BF16), vc)
        psum = psum + p

    hi = psum.astype(BF16)
    lo = (psum - hi.astype(F32)).astype(BF16)
    imp = _dot(hi, agg_ref[...]) + _dot(lo, agg_ref[...])

    sid = lax.broadcasted_iota(I32, (1, ns), 1)
    sidf = sid.astype(F32)
    cur = t // SEL_LEN
    forced = (sid == 0) | (sid == cur) | (sid == cur - 1)
    valid = sid * SEL_LEN <= t
    work = jnp.where(forced, SEL_FORCE, jnp.where(valid, imp, NEG_INF))
    picked = jnp.zeros((Q, ns), F32)
    for _ in range(min(SEL_COUNT, ns)):
        m = jnp.max(work, axis=-1, keepdims=True)
        first = jnp.min(jnp.where(work == m, sidf, float(ns)), axis=-1, keepdims=True)
        hit = sidf == first
        picked = jnp.where(hit, 1.0, picked)
        work = jnp.where(hit, PICKED, work)
    sel = jnp.where(valid, picked, 0.0)
    sel_ref[0, 0] = sel.astype(sel_ref.dtype)
    anyq = jnp.max(sel, axis=0, keepdims=True)
    tiles = _dot(jnp.broadcast_to(anyq, (8, ns)).astype(BF16), grp_ref[...])
    act_ref[0, 0, 0] = (tiles > 0.5).astype(I32)


def _cmp_to_sel_matrix(n_cmp_padded, n_sel):
    r = SEL_LEN // CMP_STRIDE
    c = CMP_LEN // CMP_STRIDE
    off = np.arange(n_cmp_padded)[:, None] - r * np.arange(n_sel)[None, :] + (c - 1)
    cnt = np.minimum(c - 1, off) - np.maximum(0, off - r + 1) + 1
    return np.clip(cnt, 0, None).astype(np.float32)


def _alibi_slopes(n_heads):
    return (2.0 ** (-8.0 * (np.arange(n_heads) + 1) / n_heads)).astype(np.float32)


def _nsa_select(P, kvc, B, S):
    G, Q = NSA_KV_GROUPS, Q_BLOCK
    nq = S // Q
    n = S // CMP_STRIDE
    ns = S // SEL_LEN
    T = B * S
    agg = jnp.asarray(_cmp_to_sel_matrix(n, ns), BF16)
    per_tile = KEY_TILE // SEL_LEN
    grp = jnp.asarray((np.arange(ns)[:, None] // per_tile) == np.arange(LANE)[None, :], BF16)
    slopes = jnp.asarray(_alibi_slopes(NSA_HEADS))
    grid_spec = pltpu.PrefetchScalarGridSpec(
        num_scalar_prefetch=1,
        grid=(B, G, nq),
        in_specs=[
            pl.BlockSpec((NSA_HPG, Q, LANE), lambda b, g, i, s_: (SLAB_NSA_Q // NSA_HPG + g, b * nq + i, 0)),
            pl.BlockSpec((1, 1, 1, n, HEAD_DIM), lambda b, g, i, s_: (0, b, g, 0, 0)),
            pl.BlockSpec((1, 1, 1, n, HEAD_DIM), lambda b, g, i, s_: (1, b, g, 0, 0)),
            pl.BlockSpec((n, ns), lambda b, g, i, s_: (0, 0)),
            pl.BlockSpec((ns, LANE), lambda b, g, i, s_: (0, 0)),
        ],
        out_specs=[
            pl.BlockSpec((Q, NSA_HPG * HEAD_DIM), lambda b, g, i, s_: (b * nq + i, g)),
            pl.BlockSpec((1, 1, Q, ns), lambda b, g, i, s_: (b, g, i, 0)),
            pl.BlockSpec((1, 1, 1, 8, LANE), lambda b, g, i, s_: (b, g, i, 0, 0)),
        ],
    )
    return pl.pallas_call(
        _nsa_select_kernel,
        out_shape=[
            jax.ShapeDtypeStruct((T, NSA_HEADS * HEAD_DIM), F32),
            jax.ShapeDtypeStruct((B, G, S, ns), BF16),
            jax.ShapeDtypeStruct((B, G, nq, 8, LANE), I32),
        ],
        grid_spec=grid_spec,
        compiler_params=_params(("parallel", "parallel", "parallel")),
        name="nsa_select",
    )(slopes, P, kvc, kvc, agg, grp)


def _nsa_attend_kernel(slopes_ref, act_ref, q_ref, ks_ref, vs_ref, kw_ref, vw_ref, sel_ref, oc_ref,
                       sm_ref, ng_ref, o_ref, m_scr, l_scr, acc_scr, *, tiles_max):
    b = pl.program_id(0)
    g = pl.program_id(1)
    blk = pl.program_id(2)
    nq = pl.num_programs(2)
    Q, H = Q_BLOCK, NSA_HPG
    ns = sel_ref.shape[3]
    per_tile = KEY_TILE // SEL_LEN
    scale = HEAD_DIM ** -0.5

    qb = q_ref[...].reshape(H * Q, HEAD_DIM).astype(BF16)
    t = blk * Q + lax.broadcasted_iota(I32, (Q, 1), 0)
    selb = sel_ref[0, 0]
    slopes = [slopes_ref[g * H + hh] for hh in range(H)]

    m_scr[...] = jnp.full_like(m_scr, NEG_INF)
    l_scr[...] = jnp.zeros_like(l_scr)
    acc_scr[...] = jnp.zeros_like(acc_scr)
    act_base = ((b * pl.num_programs(1) + g) * nq + blk) * tiles_max

    def tile_body(j, carry):
        @pl.when(act_ref[act_base + j] > 0)
        def _():
            start = pl.multiple_of(j * KEY_TILE, KEY_TILE)
            kt = ks_ref[0, pl.ds(start, KEY_TILE), :]
            vt = vs_ref[0, pl.ds(start, KEY_TILE), :]
            s = _dot_nt(qb, kt) * scale
            blk_of_key = j * per_tile + lax.broadcasted_iota(I32, (1, KEY_TILE), 1) // SEL_LEN
            expand = (lax.broadcasted_iota(I32, (ns, 1), 0) == blk_of_key).astype(BF16)
            dist = t - (start + lax.broadcasted_iota(I32, (1, KEY_TILE), 1))
            mask = (_dot(selb, expand) > 0.5) & (dist >= 0)
            distf = dist.astype(F32)
            for hh in range(H):
                rows = slice(hh * Q, (hh + 1) * Q)
                sh = jnp.where(mask, s[rows] - slopes[hh] * distf, NEG_INF)
                m_old = m_scr[rows]
                m_new = jnp.maximum(m_old, jnp.max(sh, axis=-1, keepdims=True))
                alpha = jnp.exp(m_old - m_new)
                p = jnp.where(mask, jnp.exp(sh - m_new), 0.0)
                l_scr[rows] = alpha * l_scr[rows] + jnp.sum(p, axis=-1, keepdims=True)
                acc_scr[rows] = alpha * acc_scr[rows] + _dot(p.astype(BF16), vt)
                m_scr[rows] = m_new
        return carry

    lax.fori_loop(0, (blk * Q + Q - 1) // KEY_TILE + 1, tile_body, 0)

    span = WINDOW + Q
    base = pl.multiple_of(jnp.maximum(blk * Q - WINDOW, 0), Q)
    kw = kw_ref[0, pl.ds(base, span), :]
    vw = vw_ref[0, pl.ds(base, span), :]
    sw = _dot_nt(qb, kw) * scale
    dist_w = t - (base + lax.broadcasted_iota(I32, (1, span), 1))
    mask_w = (dist_w >= 0) & (dist_w < WINDOW)
    distf_w = dist_w.astype(F32)

    sig = _sigmoid(sm_ref[0])
    lane = lax.broadcasted_iota(I32, (1, LANE), 1)
    for hh in range(H):
        rows = slice(hh * Q, (hh + 1) * Q)
        cols = slice(hh * HEAD_DIM, (hh + 1) * HEAD_DIM)
        pw = _masked_softmax(sw[rows] - slopes[hh] * distf_w, mask_w)
        o_w = _dot(pw.astype(BF16), vw)
        o_s = acc_scr[rows] / jnp.maximum(l_scr[rows], 1e-30)
        gl = LANE_GATE + (g * H + hh) * 3
        o = (_pick_lane(sig, lane, gl) * oc_ref[:, cols] + _pick_lane(sig, lane, gl + 1) * o_s
             + _pick_lane(sig, lane, gl + 2) * o_w)
        on = o * lax.rsqrt(jnp.mean(o * o, axis=-1, keepdims=True) + NORM_EPS) * ng_ref[...]
        o_ref[:, cols] = on.astype(o_ref.dtype)


def _nsa_attend(P, kv_bf16, sel, act_flat, o_c, B, S, norm_g):
    G, Q = NSA_KV_GROUPS, Q_BLOCK
    nq = S // Q
    ns = S // SEL_LEN
    T = B * S
    slopes = jnp.asarray(_alibi_slopes(NSA_HEADS))

    def kv(base):
        return pl.BlockSpec((1, S, LANE), lambda b, g, i, *_: (base + g, b, 0))

    grid_spec = pltpu.PrefetchScalarGridSpec(
        num_scalar_prefetch=2,
        grid=(B, G, nq),
        in_specs=[
            pl.BlockSpec((NSA_HPG, Q, LANE), lambda b, g, i, *_: (SLAB_NSA_Q // NSA_HPG + g, b * nq + i, 0)),
            kv(0), kv(2), kv(4), kv(6),
            pl.BlockSpec((1, 1, Q, ns), lambda b, g, i, *_: (b, g, i, 0)),
            pl.BlockSpec((Q, NSA_HPG * HEAD_DIM), lambda b, g, i, *_: (b * nq + i, g)),
            pl.BlockSpec((1, Q, LANE), lambda b, g, i, *_: (SLAB_SMALL, b * nq + i, 0)),
            pl.BlockSpec((1, LANE), lambda b, g, i, *_: (0, 0)),
        ],
        out_specs=pl.BlockSpec((Q, NSA_HPG * HEAD_DIM), lambda b, g, i, *_: (b * nq + i, g)),
        scratch_shapes=[
            pltpu.VMEM((NSA_HPG * Q, 1), F32), pltpu.VMEM((NSA_HPG * Q, 1), F32),
            pltpu.VMEM((NSA_HPG * Q, HEAD_DIM), F32),
        ],
    )
    return pl.pallas_call(
        functools.partial(_nsa_attend_kernel, tiles_max=S // KEY_TILE),
        out_shape=jax.ShapeDtypeStruct((T, NSA_HEADS * HEAD_DIM), BF16),
        grid_spec=grid_spec,
        compiler_params=_params(("parallel", "parallel", "arbitrary")),
        name="nsa_attend",
    )(slopes, act_flat, P, kv_bf16, kv_bf16, kv_bf16, kv_bf16, sel, o_c, P, norm_g.reshape(1, LANE))


def _out_proj_kernel(ydn_ref, ynsa_ref, x_ref, wo_ref, g_ref, wrh_ref, wrl_ref, br_ref,
                     x1_ref, h2_ref, ri_ref, rw_ref, cnt_ref, carry_scr):
    tm = x_ref.shape[0]
    half = ynsa_ref.shape[1]

    @pl.when(pl.program_id(0) == 0)
    def _():
        carry_scr[...] = jnp.zeros_like(carry_scr)

    ydn = jnp.concatenate([ydn_ref[hh] for hh in range(ydn_ref.shape[0])], axis=1)
    x1 = x_ref[...] + _dot(ydn, wo_ref[0:half, :]) + _dot(ynsa_ref[...], wo_ref[half:, :])
    x1_ref[...] = x1
    h2 = x1 * lax.rsqrt(jnp.mean(x1 * x1, axis=-1, keepdims=True) + NORM_EPS) * g_ref[...]
    h2_ref[...] = h2

    hi = h2.astype(BF16)
    lo = (h2 - hi.astype(F32)).astype(BF16)
    lg = _dot(hi, wrh_ref[...]) + _dot(hi, wrl_ref[...]) + _dot(lo, wrh_ref[...])
    lane = lax.broadcasted_iota(I32, (1, LANE), 1)
    lanef = lane.astype(F32)
    biased = lg + br_ref[...]

    is_grp = lane < N_GROUPS
    gmax = jnp.max(jnp.where(is_grp, lg, NEG_INF), axis=-1, keepdims=True)
    eg = jnp.where(is_grp, jnp.exp(lg - gmax), 0.0)
    pg = eg / jnp.sum(eg, axis=-1, keepdims=True)
    gb = jnp.where(is_grp, biased, NEG_INF)
    g_sel = jnp.min(jnp.where(gb == jnp.max(gb, axis=-1, keepdims=True), lanef, float(LANE)),
                    axis=-1, keepdims=True)
    p_sel = _pick_lane(pg, lanef, g_sel)

    lo_lane = N_GROUPS + g_sel * EXPERTS_PER_GROUP
    in_grp = (lanef >= lo_lane) & (lanef < lo_lane + EXPERTS_PER_GROUP)
    v1 = jnp.where(in_grp, biased, NEG_INF)
    i1 = jnp.min(jnp.where(v1 == jnp.max(v1, axis=-1, keepdims=True), lanef, float(LANE)),
                 axis=-1, keepdims=True)
    v2 = jnp.where(lanef == i1, NEG_INF * 2, v1)
    i2 = jnp.min(jnp.where(v2 == jnp.max(v2, axis=-1, keepdims=True), lanef, float(LANE)),
                 axis=-1, keepdims=True)
    el1 = _pick_lane(lg, lanef, i1)
    el2 = _pick_lane(lg, lanef, i2)
    mx = jnp.maximum(el1, el2)
    e1 = jnp.exp(el1 - mx)
    e2 = jnp.exp(el2 - mx)
    rw_ref[...] = jnp.where(lane == 0, p_sel * (e1 / (e1 + e2)),
                            jnp.where(lane == 1, p_sel * (e2 / (e1 + e2)), 0.0))

    ex1 = i1 - N_GROUPS
    ex2 = i2 - N_GROUPS
    oh1 = lanef == ex1
    oh2 = lanef == ex2
    both = jnp.where(oh1 | oh2, 1.0, 0.0)
    ri = lax.broadcasted_iota(I32, (tm, tm), 0)
    ci = lax.broadcasted_iota(I32, (tm, tm), 1)
    before = _dot((ci < ri).astype(BF16), both.astype(BF16)) + carry_scr[...]
    r1 = jnp.sum(jnp.where(oh1, before, 0.0), axis=-1, keepdims=True)
    r2 = jnp.sum(jnp.where(oh2, before, 0.0), axis=-1, keepdims=True)
    packed = jnp.where(lane == 0, ex1, jnp.where(lane == 1, ex2,
                       jnp.where(lane == 2, r1, jnp.where(lane == 3, r2, 0.0))))
    ri_ref[...] = packed.astype(I32)
    total = carry_scr[...] + jnp.sum(both, axis=0, keepdims=True)
    carry_scr[...] = total
    cnt_ref[...] = jnp.broadcast_to(total, cnt_ref.shape).astype(I32)


def _out_proj(ydn, ynsa, x2d, wo, g, wr_hi, wr_lo, br, tm=256):
    T, D = x2d.shape
    half = ynsa.shape[1]
    row = lambda i: (i, 0)
    fixed = lambda i: (0, 0)
    return pl.pallas_call(
        _out_proj_kernel,
        out_shape=[
            jax.ShapeDtypeStruct((T, D), F32), jax.ShapeDtypeStruct((T, D), F32),
            jax.ShapeDtypeStruct((T, LANE), I32), jax.ShapeDtypeStruct((T, LANE), F32),
            jax.ShapeDtypeStruct((8, LANE), I32),
        ],
        grid=(T // tm,),
        in_specs=[
            pl.BlockSpec((ydn.shape[0], tm, LANE), lambda i: (0, i, 0)),
            pl.BlockSpec((tm, half), row), pl.BlockSpec((tm, D), row),
            pl.BlockSpec((D, D), fixed), pl.BlockSpec((1, D), fixed),
            pl.BlockSpec((D, LANE), fixed), pl.BlockSpec((D, LANE), fixed), pl.BlockSpec((1, LANE), fixed),
        ],
        out_specs=[
            pl.BlockSpec((tm, D), row), pl.BlockSpec((tm, D), row),
            pl.BlockSpec((tm, LANE), row), pl.BlockSpec((tm, LANE), row), pl.BlockSpec((8, LANE), fixed),
        ],
        scratch_shapes=[pltpu.VMEM((1, LANE), F32)],
        compiler_params=_params(("arbitrary",)),
        name="out_proj_router",
    )(ydn, ynsa, x2d, wo, g, wr_hi, wr_lo, br)


def _row_copy(src_hbm, row, dst, slot, sem):
    return pltpu.make_async_copy(src_hbm.at[pl.ds(row, 1)], dst.at[pl.ds(slot, 1)], sem)


GATHER_UNROLL = 8


def _gather_rows(src_hbm, idx_ref, base, dst, sem, n_rows):
    def issue(r, c):
        _row_copy(src_hbm, idx_ref[base + r], dst, r, sem).start()
        return c
    lax.fori_loop(0, n_rows, issue, 0, unroll=GATHER_UNROLL)


def _wait_rows(src_hbm, dst, sem):
    pltpu.make_async_copy(src_hbm.at[pl.ds(0, dst.shape[0])], dst, sem).wait()


def _moe_experts_kernel(be_ref, nu_ref, rt_ref, h_hbm, wg_ref, wu_ref, wd_ref, y_ref,
                        xs_scr, wg_scr, wu_scr, wd_scr, sem):
    i = pl.program_id(0)
    R = xs_scr.shape[1]
    slot = i % 2

    @pl.when((i == 0) & (nu_ref[0] > 0))
    def _():
        _gather_rows(h_hbm, rt_ref, 0, xs_scr.at[0], sem.at[0], R)

    @pl.when(i + 1 < nu_ref[0])
    def _():
        _gather_rows(h_hbm, rt_ref, (i + 1) * R, xs_scr.at[1 - slot], sem.at[1 - slot], R)

    @pl.when(i < nu_ref[0])
    def _():
        @pl.when((i == 0) | (be_ref[jnp.maximum(i - 1, 0)] != be_ref[i]))
        def _():
            wg_scr[...] = wg_ref[0].astype(BF16)
            wu_scr[...] = wu_ref[0].astype(BF16)
            wd_scr[...] = wd_ref[0].astype(BF16)

        _wait_rows(h_hbm, xs_scr.at[slot], sem.at[slot])
        xb = xs_scr[slot].astype(BF16)
        hid = _silu(_dot(xb, wg_scr[...])) * _dot(xb, wu_scr[...])
        y_ref[...] = _dot(hid.astype(BF16), wd_scr[...])

    @pl.when(i >= nu_ref[0])
    def _():
        y_ref[...] = jnp.zeros_like(y_ref)


def _moe_experts(h2, block_expert, n_used, row_tok, w_gate, w_up, w_down):
    T, D = h2.shape
    R = MOE_ROWS
    n_blocks = block_expert.shape[0]
    DE = w_gate.shape[2]
    grid_spec = pltpu.PrefetchScalarGridSpec(
        num_scalar_prefetch=3,
        grid=(n_blocks,),
        in_specs=[
            pl.BlockSpec(memory_space=pl.ANY),
            pl.BlockSpec((1, D, DE), lambda i, be, nu, rt: (be[i], 0, 0)),
            pl.BlockSpec((1, D, DE), lambda i, be, nu, rt: (be[i], 0, 0)),
            pl.BlockSpec((1, DE, D), lambda i, be, nu, rt: (be[i], 0, 0)),
        ],
        out_specs=pl.BlockSpec((R, D), lambda i, be, nu, rt: (i, 0)),
        scratch_shapes=[
            pltpu.VMEM((2, R, D), F32), pltpu.VMEM((D, DE), BF16), pltpu.VMEM((D, DE), BF16),
            pltpu.VMEM((DE, D), BF16), pltpu.SemaphoreType.DMA((2,)),
        ],
    )
    return pl.pallas_call(
        _moe_experts_kernel,
        out_shape=jax.ShapeDtypeStruct((n_blocks * R, D), F32),
        grid_spec=grid_spec,
        compiler_params=_params(("arbitrary",)),
        name="moe_experts",
    )(block_expert, n_used, row_tok, h2, w_gate, w_up, w_down)


def _moe_combine_kernel(dest_ref, x1_ref, rw_ref, g_ref, ys_hbm, o_ref, buf, sem):
    i = pl.program_id(0)
    tm = x1_ref.shape[0]
    T = tm * pl.num_programs(0)
    slot = i % 2

    def start(step, s):
        for k in range(2):
            _gather_rows(ys_hbm, dest_ref, k * T + step * tm, buf.at[s, k], sem.at[s], tm)

    @pl.when(i == 0)
    def _():
        start(0, 0)

    @pl.when(i + 1 < pl.num_programs(0))
    def _():
        start(i + 1, 1 - slot)

    for k in range(2):
        _wait_rows(ys_hbm, buf.at[slot, k], sem.at[slot])
    rw = rw_ref[...]
    x2 = x1_ref[...] + buf[slot, 0] * rw[:, 0:1] + buf[slot, 1] * rw[:, 1:2]
    o_ref[...] = x2 * lax.rsqrt(jnp.mean(x2 * x2, axis=-1, keepdims=True) + NORM_EPS) * g_ref[...]


def _moe_combine(dest_flat, x1, rw, g, ys, tm=128):
    T, D = x1.shape
    grid_spec = pltpu.PrefetchScalarGridSpec(
        num_scalar_prefetch=1,
        grid=(T // tm,),
        in_specs=[
            pl.BlockSpec((tm, D), lambda i, d: (i, 0)),
            pl.BlockSpec((tm, LANE), lambda i, d: (i, 0)),
            pl.BlockSpec((1, D), lambda i, d: (0, 0)),
            pl.BlockSpec(memory_space=pl.ANY),
        ],
        out_specs=pl.BlockSpec((tm, D), lambda i, d: (i, 0)),
        scratch_shapes=[pltpu.VMEM((2, 2, tm, D), F32), pltpu.SemaphoreType.DMA((2,))],
    )
    return pl.pallas_call(
        _moe_combine_kernel,
        out_shape=jax.ShapeDtypeStruct((T, D), F32),
        grid_spec=grid_spec,
        compiler_params=_params(("arbitrary",)),
        name="moe_combine",
    )(dest_flat, x1, rw, g, ys)


def _slab_weights(w_in):
    D = w_in.shape[0]
    n_dn = 4 * DN_HEADS * HEAD_DIM
    n_small_a = 2 * DN_HEADS
    n_nsa = NSA_HEADS * HEAD_DIM + 6 * NSA_KV_GROUPS * HEAD_DIM
    c0 = n_dn + n_small_a
    c1 = c0 + n_nsa
    n_gate = 3 * NSA_HEADS
    pad = N_SLABS * LANE - (n_dn + n_nsa + n_small_a + n_gate)
    return jnp.concatenate(
        [w_in[:, :n_dn], w_in[:, c0:c1], w_in[:, n_dn:c0], w_in[:, c1:c1 + n_gate],
         jnp.zeros((D, pad), w_in.dtype)], axis=1).astype(BF16)


def _layer(x, norm_mix_g, w_in, conv_w, dt_bias, a_log, dn_norm_g, cmp_pos_k, cmp_w1_k, cmp_w2_k,
           cmp_pos_v, cmp_w1_v, cmp_w2_v, nsa_norm_g, w_out, norm_ffn_g, w_group, b_group,
           w_router, b_router, w_gate, w_up, w_down, norm_final_g):
    B, S, D = x.shape
    T = B * S
    x2d = x.reshape(T, D)

    P = _in_proj(x2d, norm_mix_g.reshape(1, D), _slab_weights(w_in))
    ydn = _deltanet(P, B, S, conv_w, dt_bias, a_log, dn_norm_g)

    kvc = _compress(P, B, S, jnp.stack([cmp_pos_k, cmp_pos_v]), jnp.stack([cmp_w1_k, cmp_w1_v]),
                    jnp.stack([cmp_w2_k, cmp_w2_v]))
    o_c, sel, act = _nsa_select(P, kvc, B, S)
    tiles_max = S // KEY_TILE
    act_flat = act[:, :, :, 0, :tiles_max].reshape(-1)
    kv_bf16 = P[SLAB_K_SEL:SLAB_K_SEL + 8].astype(BF16)
    ynsa = _nsa_attend(P, kv_bf16, sel, act_flat, o_c, B, S, nsa_norm_g)

    wr = jnp.zeros((D, LANE), F32).at[:, :N_GROUPS].set(w_group).at[:, N_GROUPS:N_GROUPS + N_EXPERTS].set(w_router)
    wr_hi = wr.astype(BF16)
    wr_lo = (wr - wr_hi.astype(F32)).astype(BF16)
    br = jnp.zeros((1, LANE), F32).at[0, :N_GROUPS].set(b_group).at[0, N_GROUPS:N_GROUPS + N_EXPERTS].set(b_router)
    x1, h2, ri, rw, cnt = _out_proj(ydn, ynsa, x2d, w_out.astype(BF16), norm_ffn_g.reshape(1, D),
                                    wr_hi, wr_lo, br)

    R = MOE_ROWS
    counts = cnt[0, :N_EXPERTS]
    padded = (counts + R - 1) // R * R
    pends = jnp.cumsum(padded)
    pstarts = pends - padded
    dest = pstarts[ri[:, 0:2]] + ri[:, 2:4]
    n_blocks = (2 * T + R - 1) // R + N_EXPERTS
    row_tok = jnp.zeros((n_blocks * R,), I32).at[dest.reshape(-1)].set(jnp.repeat(jnp.arange(T, dtype=I32), 2))
    block_start = jnp.arange(n_blocks, dtype=I32) * R
    block_expert = jnp.minimum(jnp.sum(block_start[:, None] >= pends[None, :], axis=1),
                               N_EXPERTS - 1).astype(I32)
    n_used = (pends[-1:] // R).astype(I32)

    ys = _moe_experts(h2, block_expert, n_used, row_tok, w_gate, w_up, w_down)
    out = _moe_combine(dest.T.reshape(-1).astype(I32), x1, rw, norm_final_g.reshape(1, D), ys)
    return out.reshape(B, S, D)


def kernel(x, norm_mix_g, w_in, conv_w, dt_bias, a_log, dn_norm_g, cmp_pos_k, cmp_w1_k, cmp_w2_k,
           cmp_pos_v, cmp_w1_v, cmp_w2_v, nsa_norm_g, w_out, norm_ffn_g, w_group, b_group,
           w_router, b_router, w_gate, w_up, w_down, norm_final_g):
    assert w_in.shape[0] == 1, "one layer"
    return _layer(x, norm_mix_g[0], w_in[0], conv_w[0], dt_bias[0], a_log[0], dn_norm_g[0],
                  cmp_pos_k[0], cmp_w1_k[0], cmp_w2_k[0], cmp_pos_v[0], cmp_w1_v[0], cmp_w2_v[0],
                  nsa_norm_g[0], w_out[0], norm_ffn_g[0], w_group[0], b_group[0], w_router[0],
                  b_router[0], w_gate[0], w_up[0], w_down[0], norm_final_g)
```

```python
import functools
import math

import numpy as np
import jax
import jax.numpy as jnp
from jax import lax
from jax.experimental import pallas as pl
from jax.experimental.pallas import tpu as pltpu

F32 = jnp.float32
BF16 = jnp.bfloat16
I32 = jnp.int32

LANE = 128
HEAD_DIM = 128
DN_HEADS = 8
NSA_HEADS = 8
NSA_KV_GROUPS = 2
NSA_HPG = NSA_HEADS // NSA_KV_GROUPS
CONV_WIDTH = 4
DN_CHUNK = 64
CMP_LEN = 32
CMP_STRIDE = 16
SEL_LEN = 64
SEL_COUNT = 16
WINDOW = 512
Q_BLOCK = 128
N_GROUPS = 8
EXPERTS_PER_GROUP = 8
N_EXPERTS = N_GROUPS * EXPERTS_PER_GROUP
NORM_EPS = 1e-6
NEG_INF = -1e30
SEL_FORCE = 1e9
PICKED = -3e38

SLAB_DN_Q, SLAB_DN_K, SLAB_DN_V, SLAB_DN_Z = 0, 8, 16, 24
SLAB_NSA_Q = 32
SLAB_K_CMP, SLAB_V_CMP, SLAB_K_SEL, SLAB_V_SEL, SLAB_K_WIN, SLAB_V_WIN = 40, 42, 44, 46, 48, 50
SLAB_SMALL = 52
N_SLABS = 54
LANE_DN_B, LANE_DN_A, LANE_GATE = 0, 8, 16

KEY_TILE = 256
MOE_ROWS = 256

VMEM_LIMIT = 56 * 1024 * 1024


def _params(sem, vmem=VMEM_LIMIT):
    return pltpu.CompilerParams(dimension_semantics=sem, vmem_limit_bytes=vmem)


def _dot(a, b):
    return jnp.dot(a, b, preferred_element_type=F32)


def _dot_nt(a, b):
    return lax.dot_general(a, b, (((1,), (1,)), ((), ())), preferred_element_type=F32)


def _dot_exact(a, b):
    return jnp.dot(a, b, preferred_element_type=F32, precision=lax.Precision.HIGHEST)


def _sigmoid(x):
    return 1.0 / (1.0 + jnp.exp(-x))


def _silu(x):
    return x * _sigmoid(x)


def _pick_lane(x, lane_ids, idx):
    return jnp.sum(jnp.where(lane_ids == idx, x, 0.0), axis=1, keepdims=True)


def _in_proj_kernel(x_ref, g_ref, w_ref, o_ref, h_scr):
    @pl.when(pl.program_id(1) == 0)
    def _():
        xf = x_ref[...]
        ms = jnp.mean(xf * xf, axis=-1, keepdims=True)
        h_scr[...] = (xf * lax.rsqrt(ms + NORM_EPS) * g_ref[...]).astype(BF16)

    acc = _dot(h_scr[...], w_ref[...])
    for k in range(o_ref.shape[0]):
        o_ref[k] = acc[:, k * LANE:(k + 1) * LANE]


def _in_proj(x2d, g, w_slabs, tm=1024, tn=768):
    T, D = x2d.shape
    NP = w_slabs.shape[1]
    ns = tn // LANE
    return pl.pallas_call(
        _in_proj_kernel,
        out_shape=jax.ShapeDtypeStruct((NP // LANE, T, LANE), F32),
        grid=(T // tm, NP // tn),
        in_specs=[
            pl.BlockSpec((tm, D), lambda i, j: (i, 0)),
            pl.BlockSpec((1, D), lambda i, j: (0, 0)),
            pl.BlockSpec((D, tn), lambda i, j: (0, j)),
        ],
        out_specs=pl.BlockSpec((ns, tm, LANE), lambda i, j: (j, i, 0)),
        scratch_shapes=[pltpu.VMEM((tm, D), BF16)],
        compiler_params=_params(("parallel", "arbitrary")),
        name="in_proj",
    )(x2d, g, w_slabs)


DN_HEADS_PER_ITER = 4


def _deltanet_kernel(p_ref, sm_ref, cw_ref, alog_ref, dtb_ref, ng_ref, o_ref,
                     ext_scr, s_scr, gc_scr, gl_scr, gct_scr, beta_scr):
    L = p_ref.shape[1]
    C = DN_CHUNK
    H = DN_HEADS

    @pl.when(pl.program_id(1) == 0)
    def _():
        ext_scr[:, 0:8, :] = jnp.zeros((3 * H, 8, LANE), F32)
        s_scr[...] = jnp.zeros_like(s_scr)

    sm = sm_ref[0]
    lane = lax.broadcasted_iota(I32, (1, LANE), 1)
    xa = sm + dtb_ref[...]
    softplus = jnp.maximum(xa, 0.0) + jnp.log1p(jnp.exp(-jnp.abs(xa)))
    ld = -jnp.exp(alog_ref[...]) * softplus
    ri = lax.broadcasted_iota(I32, (L, L), 0)
    ci = lax.broadcasted_iota(I32, (L, L), 1)
    same = (ri // C) == (ci // C)
    causal = same & (ci <= ri)
    gc_slab = _dot_exact(jnp.where(causal, 1.0, 0.0), ld)
    gc_scr[...] = gc_slab
    gct_scr[...] = gc_slab.T
    gl_scr[...] = _dot_exact(jnp.where(same, 1.0, 0.0), ld)
    beta_scr[...] = _sigmoid(sm)

    def conv_silu(slab, h):
        idx = slab + h
        ext_scr[idx, 8:, :] = p_ref[idx]
        w = cw_ref[idx]
        acc = ext_scr[idx, pl.ds(8 - (CONV_WIDTH - 1), L), :] * w[0:1]
        for j in range(1, CONV_WIDTH):
            acc = acc + ext_scr[idx, pl.ds(8 - (CONV_WIDTH - 1) + j, L), :] * w[j:j + 1]
        ext_scr[idx, 0:8, :] = ext_scr[idx, L:L + 8, :]
        return _silu(acc)

    def l2n(x):
        return x * lax.rsqrt(jnp.sum(x * x, axis=-1, keepdims=True) + NORM_EPS)

    def head_group(hg, carry):
        hs = [hg * DN_HEADS_PER_ITER + dh for dh in range(DN_HEADS_PER_ITER)]
        n = range(len(hs))
        qs = [l2n(conv_silu(SLAB_DN_Q, h)) * (HEAD_DIM ** -0.5) for h in hs]
        ks = [l2n(conv_silu(SLAB_DN_K, h)) for h in hs]
        vs = [conv_silu(SLAB_DN_V, h) for h in hs]
        betas = [_pick_lane(beta_scr[...], lane, LANE_DN_B + h) for h in hs]
        gcols = [_pick_lane(gc_scr[...], lane, LANE_DN_A + h) for h in hs]
        glasts = [_pick_lane(gl_scr[...], lane, LANE_DN_A + h) for h in hs]
        grows = [gct_scr[pl.ds(LANE_DN_A + h, 1), :] for h in hs]
        states = [s_scr[h] for h in hs]
        zs = [p_ref[SLAB_DN_Z + h] for h in hs]

        decays = [jnp.where(causal, jnp.exp(jnp.where(causal, gcols[i] - grows[i], 0.0)), 0.0) for i in n]
        egcs = [jnp.exp(gcols[i]) for i in n]
        kbs = [ks[i] * betas[i] for i in n]
        kts = [ks[i].astype(BF16) for i in n]
        lms = [jnp.where(ci < ri, _dot_nt(kbs[i].astype(BF16), kts[i]) * decays[i], 0.0) for i in n]
        tinvs = [jnp.where(ri == ci, 1.0, 0.0) - lms[i] for i in n]
        pws = lms
        for _ in range(int(math.log2(C)) - 1):
            pwbs = [pws[i].astype(BF16) for i in n]
            pws = [_dot(pwbs[i], pwbs[i]) for i in n]
            tinvs = [tinvs[i] + _dot(tinvs[i].astype(BF16), pws[i].astype(BF16)) for i in n]
        tbs = [tinvs[i].astype(BF16) for i in n]
        us = [_dot(tbs[i], (vs[i] * betas[i]).astype(BF16)) for i in n]
        ws = [_dot(tbs[i], (kbs[i] * egcs[i]).astype(BF16)) for i in n]
        qks = [_dot_nt(qs[i].astype(BF16), kts[i]) * decays[i] for i in n]
        q_decs = [(qs[i] * egcs[i]).astype(BF16) for i in n]
        k_dec_ts = [(ks[i] * jnp.exp(glasts[i] - gcols[i])).T for i in n]

        outs = [[] for _ in n]
        for c in range(L // C):
            sl = slice(c * C, (c + 1) * C)
            sbs = [states[i].astype(BF16) for i in n]
            vnbs = [(us[i][sl] - _dot(ws[i][sl].astype(BF16), sbs[i])).astype(BF16) for i in n]
            for i in n:
                outs[i].append(_dot(q_decs[i][sl], sbs[i]) + _dot(qks[i][sl, sl].astype(BF16), vnbs[i]))
            states = [states[i] * jnp.exp(glasts[i][c * C:c * C + 1, :])
                      + _dot(k_dec_ts[i][:, sl].astype(BF16), vnbs[i]) for i in n]

        for i, h in enumerate(hs):
            s_scr[h] = states[i]
            o = jnp.concatenate(outs[i], axis=0)
            on = o * lax.rsqrt(jnp.mean(o * o, axis=-1, keepdims=True) + NORM_EPS) * ng_ref[...]
            o_ref[h] = (on * _silu(zs[i])).astype(o_ref.dtype)
        return carry

    lax.fori_loop(0, H // DN_HEADS_PER_ITER, head_group, 0)


def _deltanet(P, B, S, conv_w, dt_bias, a_log, norm_g, L=256):
    T = B * S
    nL = S // L
    H = DN_HEADS
    cw = conv_w.reshape(CONV_WIDTH, 3 * H, LANE).transpose(1, 0, 2)
    alog = jnp.zeros((1, LANE), F32).at[0, LANE_DN_A:LANE_DN_A + H].set(a_log)
    dtb = jnp.zeros((1, LANE), F32).at[0, LANE_DN_A:LANE_DN_A + H].set(dt_bias)
    vec = pl.BlockSpec((1, LANE), lambda b, i: (0, 0))
    return pl.pallas_call(
        _deltanet_kernel,
        out_shape=jax.ShapeDtypeStruct((H, T, LANE), BF16),
        grid=(B, nL),
        in_specs=[
            pl.BlockSpec((4 * H, L, LANE), lambda b, i: (0, b * nL + i, 0)),
            pl.BlockSpec((1, L, LANE), lambda b, i: (SLAB_SMALL, b * nL + i, 0)),
            pl.BlockSpec((3 * H, CONV_WIDTH, LANE), lambda b, i: (0, 0, 0)),
            vec, vec, vec,
        ],
        out_specs=pl.BlockSpec((H, L, LANE), lambda b, i: (0, b * nL + i, 0)),
        scratch_shapes=[
            pltpu.VMEM((3 * H, L + 8, LANE), F32), pltpu.VMEM((H, HEAD_DIM, HEAD_DIM), F32),
            pltpu.VMEM((L, LANE), F32), pltpu.VMEM((L, LANE), F32), pltpu.VMEM((LANE, L), F32),
            pltpu.VMEM((L, LANE), F32),
        ],
        compiler_params=_params(("parallel", "arbitrary")),
        name="deltanet",
    )(P, P, cw, alog, dtb, norm_g.reshape(1, LANE))


def _gelu_tanh(x):
    return 0.5 * x * (1.0 + jnp.tanh(math.sqrt(2.0 / math.pi) * (x + 0.044715 * (x * x * x))))


def _compress_kernel(x_ref, pos_ref, w1_ref, w2_ref, o_ref, ot_ref):
    n = o_ref.shape[3]
    half = CMP_LEN // 2
    y1 = jnp.zeros((n, HEAD_DIM), F32)
    y2 = jnp.zeros((n, HEAD_DIM), F32)
    for l in range(half):
        xl = x_ref[0, pl.ds(l, n, stride=CMP_STRIDE), :]
        y1 = y1 + _dot((xl + pos_ref[0, l:l + 1, :]).astype(BF16), w1_ref[0, l].astype(BF16))
        y2 = y2 + _dot((xl + pos_ref[0, half + l:half + l + 1, :]).astype(BF16),
                       w1_ref[0, half + l].astype(BF16))
    hid = _gelu_tanh(y1 + pltpu.roll(y2, n - 1, 0))
    row = lax.broadcasted_iota(I32, (n, 1), 0)
    hb = jnp.where(row < n - 1, hid, 0.0).astype(BF16)
    w2 = w2_ref[0]
    o_ref[0, 0, 0] = _dot(hb, w2.astype(BF16)).astype(o_ref.dtype)
    ot_ref[0, 0, 0] = _dot_nt(w2.T.astype(BF16), hb).astype(ot_ref.dtype)


def _compress(P, B, S, pos, w1, w2):
    n = S // CMP_STRIDE
    G = NSA_KV_GROUPS
    return pl.pallas_call(
        _compress_kernel,
        out_shape=[jax.ShapeDtypeStruct((2, B, G, n, HEAD_DIM), BF16),
                   jax.ShapeDtypeStruct((2, B, G, HEAD_DIM, n), BF16)],
        grid=(2, B, G),
        in_specs=[
            pl.BlockSpec((1, S, LANE), lambda c, b, g: (SLAB_K_CMP + 2 * c + g, b, 0)),
            pl.BlockSpec((1, CMP_LEN, HEAD_DIM), lambda c, b, g: (c, 0, 0)),
            pl.BlockSpec((1, CMP_LEN, HEAD_DIM, HEAD_DIM), lambda c, b, g: (c, 0, 0, 0)),
            pl.BlockSpec((1, HEAD_DIM, HEAD_DIM), lambda c, b, g: (c, 0, 0)),
        ],
        out_specs=[pl.BlockSpec((1, 1, 1, n, HEAD_DIM), lambda c, b, g: (c, b, g, 0, 0)),
                   pl.BlockSpec((1, 1, 1, HEAD_DIM, n), lambda c, b, g: (c, b, g, 0, 0))],
        compiler_params=_params(("parallel", "parallel", "parallel")),
        name="compress",
    )(P, pos, w1, w2)


LOG2E = math.log2(math.e)
M_INIT = -1e29


def _q_transposed(q_ref):
    c1 = (HEAD_DIM ** -0.5) * LOG2E
    return jnp.concatenate([(q_ref[hh] * c1).T for hh in range(q_ref.shape[0])], axis=1).astype(BF16)


def _nsa_select_kernel(slopes_ref, q_ref, kc_ref, vct_ref, aggt_ref, grp_ref, oct_ref, selt_ref, act_ref):
    g = pl.program_id(1)
    blk = pl.program_id(2)
    Q, H = Q_BLOCK, NSA_HPG
    n = kc_ref.shape[3]
    ns = selt_ref.shape[3]
    st = _dot(kc_ref[0, 0, 0], _q_transposed(q_ref))
    vct = vct_ref[0, 0, 0]

    t = blk * Q + lax.broadcasted_iota(I32, (1, Q), 1)
    cpos = lax.broadcasted_iota(I32, (n, Q), 0) * CMP_STRIDE + (CMP_LEN - 1)
    valid = cpos <= t
    rel = (cpos - (blk * Q + Q - 1)).astype(F32)

    psum = jnp.zeros((n, Q), F32)
    for hh in range(H):
        sh = jnp.where(valid, st[:, hh * Q:(hh + 1) * Q] + rel * (slopes_ref[g * H + hh] * LOG2E), NEG_INF)
        e = jnp.where(valid, jnp.exp2(sh - jnp.max(sh, axis=0, keepdims=True)), 0.0)
        p = e * (1.0 / jnp.maximum(jnp.sum(e, axis=0, keepdims=True), 1e-30))
        oct_ref[0, 0, 0, hh * HEAD_DIM:(hh + 1) * HEAD_DIM, :] = _dot(vct, p.astype(BF16))
        psum = psum + p

    hi = psum.astype(BF16)
    lo = (psum - hi.astype(F32)).astype(BF16)
    imp = _dot(aggt_ref[...], hi) + _dot(aggt_ref[...], lo)

    sid = lax.broadcasted_iota(I32, (ns, Q), 0)
    sidf = sid.astype(F32)
    cur = t // SEL_LEN
    forced = (sid == 0) | (sid == cur) | (sid == cur - 1)
    valid_s = sid * SEL_LEN <= t
    work = jnp.where(forced, SEL_FORCE, jnp.where(valid_s, imp, NEG_INF))
    picked = jnp.zeros((ns, Q), F32)
    for _ in range(min(SEL_COUNT, ns)):
        m = jnp.max(work, axis=0, keepdims=True)
        first = jnp.min(jnp.where(work == m, sidf, float(ns)), axis=0, keepdims=True)
        hit = sidf == first
        picked = jnp.where(hit, 1.0, picked)
        work = jnp.where(hit, PICKED, work)
    selt = jnp.where(valid_s, picked, 0.0)
    selt_ref[0, 0, 0] = selt
    per_block = _dot_nt(jnp.ones((8, Q), BF16), selt.astype(BF16))
    tiles = _dot(per_block.astype(BF16), grp_ref[...])
    act_ref[0, 0, 0] = (tiles > 0.5).astype(I32)


def _cmp_to_sel_matrix(n_cmp_padded, n_sel):
    r = SEL_LEN // CMP_STRIDE
    c = CMP_LEN // CMP_STRIDE
    off = np.arange(n_cmp_padded)[:, None] - r * np.arange(n_sel)[None, :] + (c - 1)
    cnt = np.minimum(c - 1, off) - np.maximum(0, off - r + 1) + 1
    return np.clip(cnt, 0, None).astype(np.float32)


def _alibi_slopes(n_heads):
    return (2.0 ** (-8.0 * (np.arange(n_heads) + 1) / n_heads)).astype(np.float32)


def _nsa_select(P, kvc, kvct, B, S):
    G, Q, H = NSA_KV_GROUPS, Q_BLOCK, NSA_HPG
    nq = S // Q
    n = S // CMP_STRIDE
    ns = S // SEL_LEN
    aggt = jnp.asarray(_cmp_to_sel_matrix(n, ns).T, BF16)
    per_tile = KEY_TILE // SEL_LEN
    grp = jnp.asarray((np.arange(ns)[:, None] // per_tile) == np.arange(LANE)[None, :], BF16)
    slopes = jnp.asarray(_alibi_slopes(NSA_HEADS))
    per_blk = lambda b, g, i, s_: (b, g, i, 0, 0)
    grid_spec = pltpu.PrefetchScalarGridSpec(
        num_scalar_prefetch=1,
        grid=(B, G, nq),
        in_specs=[
            pl.BlockSpec((H, Q, LANE), lambda b, g, i, s_: (SLAB_NSA_Q // H + g, b * nq + i, 0)),
            pl.BlockSpec((1, 1, 1, n, HEAD_DIM), lambda b, g, i, s_: (0, b, g, 0, 0)),
            pl.BlockSpec((1, 1, 1, HEAD_DIM, n), lambda b, g, i, s_: (1, b, g, 0, 0)),
            pl.BlockSpec((ns, n), lambda b, g, i, s_: (0, 0)),
            pl.BlockSpec((ns, LANE), lambda b, g, i, s_: (0, 0)),
        ],
        out_specs=[
            pl.BlockSpec((1, 1, 1, H * HEAD_DIM, Q), per_blk),
            pl.BlockSpec((1, 1, 1, ns, Q), per_blk),
            pl.BlockSpec((1, 1, 1, 8, LANE), per_blk),
        ],
    )
    return pl.pallas_call(
        _nsa_select_kernel,
        out_shape=[
            jax.ShapeDtypeStruct((B, G, nq, H * HEAD_DIM, Q), F32),
            jax.ShapeDtypeStruct((B, G, nq, ns, Q), F32),
            jax.ShapeDtypeStruct((B, G, nq, 8, LANE), I32),
        ],
        grid_spec=grid_spec,
        compiler_params=_params(("parallel", "parallel", "parallel")),
        name="nsa_select",
    )(slopes, P, kvc, kvct, aggt, grp)


def _nsa_attend_kernel(slopes_ref, act_ref, q_ref, ks_ref, vst_ref, kw_ref, vwt_ref, selt_ref, oct_ref,
                       sm_ref, ng_ref, o_ref, m_scr, l_scr, acc_scr, sig_scr, *, tiles_max):
    b = pl.program_id(0)
    g = pl.program_id(1)
    blk = pl.program_id(2)
    nq = pl.num_programs(2)
    Q, H = Q_BLOCK, NSA_HPG
    per_tile = KEY_TILE // SEL_LEN

    def per_head(x):
        return jnp.concatenate([x] * H, axis=1)

    qt = _q_transposed(q_ref)
    t = blk * Q + lax.broadcasted_iota(I32, (1, Q), 1)
    t_last = blk * Q + Q - 1
    slope_row = jnp.concatenate(
        [jnp.full((1, Q), slopes_ref[g * H + hh] * LOG2E, F32) for hh in range(H)], axis=1)
    key_in_tile = lax.broadcasted_iota(I32, (KEY_TILE, Q), 0)
    diag = blk // (KEY_TILE // Q)

    def reset():
        m_scr[...] = jnp.full_like(m_scr, M_INIT)
        l_scr[...] = jnp.zeros_like(l_scr)
        acc_scr[...] = jnp.zeros_like(acc_scr)

    def tile(k_ref, vt_ref, j, mask_fn):
        start = pl.multiple_of(j * KEY_TILE, KEY_TILE)
        pos = start + key_in_tile
        s = _dot(k_ref[0, pl.ds(start, KEY_TILE), :], qt) + per_head((pos - t_last).astype(F32)) * slope_row
        s = jnp.where(per_head(mask_fn(pos)), s, NEG_INF)
        m_old = m_scr[...]
        m_new = jnp.maximum(m_old, jnp.max(s, axis=0, keepdims=True))
        alpha = jnp.exp2(m_old - m_new)
        p = jnp.exp2(s - m_new)
        l_scr[...] = alpha * l_scr[...] + jnp.sum(p, axis=0, keepdims=True)
        acc_scr[...] = alpha * acc_scr[...] + _dot(vt_ref[0, 0, j], p.astype(BF16))
        m_scr[...] = m_new

    def finish():
        return acc_scr[...] * (1.0 / jnp.maximum(l_scr[...], 1e-30))

    def selected(j):
        rows8 = selt_ref[0, 0, 0, pl.ds(pl.multiple_of((j // 2) * 8, 8), 8), :]
        rows = jnp.where(j % 2 == 0, rows8[0:per_tile], rows8[per_tile:2 * per_tile])
        return jnp.concatenate(
            [jnp.broadcast_to(rows[c:c + 1], (SEL_LEN, Q)) for c in range(per_tile)], axis=0)

    reset()
    act_base = ((b * pl.num_programs(1) + g) * nq + blk) * tiles_max

    def tile_body(j, carry):
        @pl.when(act_ref[act_base + j] > 0)
        def _():
            sel_j = selected(j)
            tile(ks_ref, vst_ref, j, lambda pos: sel_j > 0.5)
        return carry

    lax.fori_loop(0, diag, tile_body, 0)
    sel_d = selected(diag)
    tile(ks_ref, vst_ref, diag, lambda pos: jnp.where(pos <= t, sel_d, 0.0) > 0.5)
    o_sel = finish()

    def in_window(pos):
        return jnp.where(pos <= t, t - pos, WINDOW) < WINDOW

    reset()
    for back in range(WINDOW // KEY_TILE, 0, -1):
        @pl.when(diag - back >= 0)
        def _():
            tile(kw_ref, vwt_ref, jnp.maximum(diag - back, 0), in_window)
    tile(kw_ref, vwt_ref, diag, in_window)
    o_win = finish()

    sig_scr[...] = _sigmoid(sm_ref[0].T)
    for hh in range(H):
        cols = slice(hh * Q, (hh + 1) * Q)
        gl = LANE_GATE + (g * H + hh) * 3
        o = (sig_scr[pl.ds(gl, 1), :] * oct_ref[0, 0, 0, hh * HEAD_DIM:(hh + 1) * HEAD_DIM, :]
             + sig_scr[pl.ds(gl + 1, 1), :] * o_sel[:, cols] + sig_scr[pl.ds(gl + 2, 1), :] * o_win[:, cols])
        on = o * lax.rsqrt(jnp.mean(o * o, axis=0, keepdims=True) + NORM_EPS) * ng_ref[...]
        o_ref[:, hh * HEAD_DIM:(hh + 1) * HEAD_DIM] = on.T.astype(o_ref.dtype)


def _nsa_attend(P, k_bf16, vt_tiles, selt, act_flat, oct, B, S, norm_g):
    G, Q, H = NSA_KV_GROUPS, Q_BLOCK, NSA_HPG
    nq = S // Q
    ns = S // SEL_LEN
    nt = S // KEY_TILE
    T = B * S
    slopes = jnp.asarray(_alibi_slopes(NSA_HEADS))
    ng = jnp.broadcast_to(norm_g[:, None], (HEAD_DIM, Q))
    per_blk = lambda b, g, i, *_: (b, g, i, 0, 0)

    def k_spec(base):
        return pl.BlockSpec((1, S, LANE), lambda b, g, i, *_: (base + g, b, 0))

    def vt_spec(base):
        return pl.BlockSpec((1, 1, nt, HEAD_DIM, KEY_TILE), lambda b, g, i, *_: (base + g, b, 0, 0, 0))

    grid_spec = pltpu.PrefetchScalarGridSpec(
        num_scalar_prefetch=2,
        grid=(B, G, nq),
        in_specs=[
            pl.BlockSpec((H, Q, LANE), lambda b, g, i, *_: (SLAB_NSA_Q // H + g, b * nq + i, 0)),
            k_spec(0), vt_spec(0), k_spec(G), vt_spec(G),
            pl.BlockSpec((1, 1, 1, ns, Q), per_blk),
            pl.BlockSpec((1, 1, 1, H * HEAD_DIM, Q), per_blk),
            pl.BlockSpec((1, Q, LANE), lambda b, g, i, *_: (SLAB_SMALL, b * nq + i, 0)),
            pl.BlockSpec((HEAD_DIM, Q), lambda b, g, i, *_: (0, 0)),
        ],
        out_specs=pl.BlockSpec((Q, H * HEAD_DIM), lambda b, g, i, *_: (b * nq + i, g)),
        scratch_shapes=[
            pltpu.VMEM((1, H * Q), F32), pltpu.VMEM((1, H * Q), F32),
            pltpu.VMEM((HEAD_DIM, H * Q), F32), pltpu.VMEM((LANE, Q), F32),
        ],
    )
    return pl.pallas_call(
        functools.partial(_nsa_attend_kernel, tiles_max=nt),
        out_shape=jax.ShapeDtypeStruct((T, NSA_HEADS * HEAD_DIM), BF16),
        grid_spec=grid_spec,
        compiler_params=_params(("parallel", "parallel", "arbitrary")),
        name="nsa_attend",
    )(slopes, act_flat, P, k_bf16, vt_tiles, k_bf16, vt_tiles, selt, oct, P, ng)


def _attend_operands(P, B, S):
    G = NSA_KV_GROUPS
    nt = S // KEY_TILE
    k_bf16 = jnp.concatenate([P[SLAB_K_SEL:SLAB_K_SEL + G], P[SLAB_K_WIN:SLAB_K_WIN + G]]).astype(BF16)
    v = jnp.concatenate([P[SLAB_V_SEL:SLAB_V_SEL + G], P[SLAB_V_WIN:SLAB_V_WIN + G]]).astype(BF16)
    vt_tiles = v.reshape(2 * G, B, nt, KEY_TILE, HEAD_DIM).transpose(0, 1, 2, 4, 3)
    return k_bf16, vt_tiles


def _active_tiles(act, S):
    return act[:, :, :, 0, :S // KEY_TILE].reshape(-1)


def _out_proj_kernel(ydn_ref, ynsa_ref, x_ref, wo_ref, g_ref, wrh_ref, wrl_ref, br_ref,
                     x1_ref, h2_ref, ri_ref, rw_ref, cnt_ref, carry_scr):
    tm = x_ref.shape[0]
    half = ynsa_ref.shape[1]

    @pl.when(pl.program_id(0) == 0)
    def _():
        carry_scr[...] = jnp.zeros_like(carry_scr)

    ydn = jnp.concatenate([ydn_ref[hh] for hh in range(ydn_ref.shape[0])], axis=1)
    x1 = x_ref[...] + _dot(ydn, wo_ref[0:half, :]) + _dot(ynsa_ref[...], wo_ref[half:, :])
    x1_ref[...] = x1
    h2 = x1 * lax.rsqrt(jnp.mean(x1 * x1, axis=-1, keepdims=True) + NORM_EPS) * g_ref[...]
    h2_ref[...] = h2

    hi = h2.astype(BF16)
    lo = (h2 - hi.astype(F32)).astype(BF16)
    lg = _dot(hi, wrh_ref[...]) + _dot(hi, wrl_ref[...]) + _dot(lo, wrh_ref[...])
    lane = lax.broadcasted_iota(I32, (1, LANE), 1)
    lanef = lane.astype(F32)
    biased = lg + br_ref[...]

    is_grp = lane < N_GROUPS
    gmax = jnp.max(jnp.where(is_grp, lg, NEG_INF), axis=-1, keepdims=True)
    eg = jnp.where(is_grp, jnp.exp(lg - gmax), 0.0)
    pg = eg / jnp.sum(eg, axis=-1, keepdims=True)
    gb = jnp.where(is_grp, biased, NEG_INF)
    g_sel = jnp.min(jnp.where(gb == jnp.max(gb, axis=-1, keepdims=True), lanef, float(LANE)),
                    axis=-1, keepdims=True)
    p_sel = _pick_lane(pg, lanef, g_sel)

    lo_lane = N_GROUPS + g_sel * EXPERTS_PER_GROUP
    in_grp = (lanef >= lo_lane) & (lanef < lo_lane + EXPERTS_PER_GROUP)
    v1 = jnp.where(in_grp, biased, NEG_INF)
    i1 = jnp.min(jnp.where(v1 == jnp.max(v1, axis=-1, keepdims=True), lanef, float(LANE)),
                 axis=-1, keepdims=True)
    v2 = jnp.where(lanef == i1, NEG_INF * 2, v1)
    i2 = jnp.min(jnp.where(v2 == jnp.max(v2, axis=-1, keepdims=True), lanef, float(LANE)),
                 axis=-1, keepdims=True)
    el1 = _pick_lane(lg, lanef, i1)
    el2 = _pick_lane(lg, lanef, i2)
    mx = jnp.maximum(el1, el2)
    e1 = jnp.exp(el1 - mx)
    e2 = jnp.exp(el2 - mx)
    rw_ref[...] = jnp.where(lane == 0, p_sel * (e1 / (e1 + e2)),
                            jnp.where(lane == 1, p_sel * (e2 / (e1 + e2)), 0.0))

    ex1 = i1 - N_GROUPS
    ex2 = i2 - N_GROUPS
    oh1 = lanef == ex1
    oh2 = lanef == ex2
    both = jnp.where(oh1 | oh2, 1.0, 0.0)
    ri = lax.broadcasted_iota(I32, (tm, tm), 0)
    ci = lax.broadcasted_iota(I32, (tm, tm), 1)
    before = _dot((ci < ri).astype(BF16), both.astype(BF16)) + carry_scr[...]
    r1 = jnp.sum(jnp.where(oh1, before, 0.0), axis=-1, keepdims=True)
    r2 = jnp.sum(jnp.where(oh2, before, 0.0), axis=-1, keepdims=True)
    packed = jnp.where(lane == 0, ex1, jnp.where(lane == 1, ex2,
                       jnp.where(lane == 2, r1, jnp.where(lane == 3, r2, 0.0))))
    ri_ref[...] = packed.astype(I32)
    total = carry_scr[...] + jnp.sum(both, axis=0, keepdims=True)
    carry_scr[...] = total
    cnt_ref[...] = jnp.broadcast_to(total, cnt_ref.shape).astype(I32)


def _out_proj(ydn, ynsa, x2d, wo, g, wr_hi, wr_lo, br, tm=256):
    T, D = x2d.shape
    half = ynsa.shape[1]
    row = lambda i: (i, 0)
    fixed = lambda i: (0, 0)
    return pl.pallas_call(
        _out_proj_kernel,
        out_shape=[
            jax.ShapeDtypeStruct((T, D), F32), jax.ShapeDtypeStruct((T, D), F32),
            jax.ShapeDtypeStruct((T, LANE), I32), jax.ShapeDtypeStruct((T, LANE), F32),
            jax.ShapeDtypeStruct((8, LANE), I32),
        ],
        grid=(T // tm,),
        in_specs=[
            pl.BlockSpec((ydn.shape[0], tm, LANE), lambda i: (0, i, 0)),
            pl.BlockSpec((tm, half), row), pl.BlockSpec((tm, D), row),
            pl.BlockSpec((D, D), fixed), pl.BlockSpec((1, D), fixed),
            pl.BlockSpec((D, LANE), fixed), pl.BlockSpec((D, LANE), fixed), pl.BlockSpec((1, LANE), fixed),
        ],
        out_specs=[
            pl.BlockSpec((tm, D), row), pl.BlockSpec((tm, D), row),
            pl.BlockSpec((tm, LANE), row), pl.BlockSpec((tm, LANE), row), pl.BlockSpec((8, LANE), fixed),
        ],
        scratch_shapes=[pltpu.VMEM((1, LANE), F32)],
        compiler_params=_params(("arbitrary",)),
        name="out_proj_router",
    )(ydn, ynsa, x2d, wo, g, wr_hi, wr_lo, br)


def _row_copy(src_hbm, row, dst, slot, sem):
    return pltpu.make_async_copy(src_hbm.at[pl.ds(row, 1)], dst.at[pl.ds(slot, 1)], sem)


GATHER_UNROLL = 8


def _gather_rows(src_hbm, idx_ref, base, dst, sem, n_rows):
    def issue(r, c):
        _row_copy(src_hbm, idx_ref[base + r], dst, r, sem).start()
        return c
    lax.fori_loop(0, n_rows, issue, 0, unroll=GATHER_UNROLL)


def _wait_rows(src_hbm, dst, sem):
    pltpu.make_async_copy(src_hbm.at[pl.ds(0, dst.shape[0])], dst, sem).wait()


def _moe_experts_kernel(be_ref, nu_ref, rt_ref, h_hbm, wg_ref, wu_ref, wd_ref, y_ref,
                        xs_scr, wg_scr, wu_scr, wd_scr, sem):
    i = pl.program_id(0)
    R = xs_scr.shape[1]
    slot = i % 2

    @pl.when((i == 0) & (nu_ref[0] > 0))
    def _():
        _gather_rows(h_hbm, rt_ref, 0, xs_scr.at[0], sem.at[0], R)

    @pl.when(i + 1 < nu_ref[0])
    def _():
        _gather_rows(h_hbm, rt_ref, (i + 1) * R, xs_scr.at[1 - slot], sem.at[1 - slot], R)

    @pl.when(i < nu_ref[0])
    def _():
        @pl.when((i == 0) | (be_ref[jnp.maximum(i - 1, 0)] != be_ref[i]))
        def _():
            wg_scr[...] = wg_ref[0].astype(BF16)
            wu_scr[...] = wu_ref[0].astype(BF16)
            wd_scr[...] = wd_ref[0].astype(BF16)

        _wait_rows(h_hbm, xs_scr.at[slot], sem.at[slot])
        xb = xs_scr[slot].astype(BF16)
        hid = _silu(_dot(xb, wg_scr[...])) * _dot(xb, wu_scr[...])
        y_ref[...] = _dot(hid.astype(BF16), wd_scr[...])

    @pl.when(i >= nu_ref[0])
    def _():
        y_ref[...] = jnp.zeros_like(y_ref)


def _moe_experts(h2, block_expert, n_used, row_tok, w_gate, w_up, w_down):
    T, D = h2.shape
    R = MOE_ROWS
    n_blocks = block_expert.shape[0]
    DE = w_gate.shape[2]
    grid_spec = pltpu.PrefetchScalarGridSpec(
        num_scalar_prefetch=3,
        grid=(n_blocks,),
        in_specs=[
            pl.BlockSpec(memory_space=pl.ANY),
            pl.BlockSpec((1, D, DE), lambda i, be, nu, rt: (be[i], 0, 0)),
            pl.BlockSpec((1, D, DE), lambda i, be, nu, rt: (be[i], 0, 0)),
            pl.BlockSpec((1, DE, D), lambda i, be, nu, rt: (be[i], 0, 0)),
        ],
        out_specs=pl.BlockSpec((R, D), lambda i, be, nu, rt: (i, 0)),
        scratch_shapes=[
            pltpu.VMEM((2, R, D), F32), pltpu.VMEM((D, DE), BF16), pltpu.VMEM((D, DE), BF16),
            pltpu.VMEM((DE, D), BF16), pltpu.SemaphoreType.DMA((2,)),
        ],
    )
    return pl.pallas_call(
        _moe_experts_kernel,
        out_shape=jax.ShapeDtypeStruct((n_blocks * R, D), F32),
        grid_spec=grid_spec,
        compiler_params=_params(("arbitrary",)),
        name="moe_experts",
    )(block_expert, n_used, row_tok, h2, w_gate, w_up, w_down)


def _moe_combine_kernel(dest_ref, x1_ref, rw_ref, g_ref, ys_hbm, o_ref, buf, sem):
    i = pl.program_id(0)
    tm = x1_ref.shape[0]
    T = tm * pl.num_programs(0)
    slot = i % 2

    def start(step, s):
        for k in range(2):
            _gather_rows(ys_hbm, dest_ref, k * T + step * tm, buf.at[s, k], sem.at[s], tm)

    @pl.when(i == 0)
    def _():
        start(0, 0)

    @pl.when(i + 1 < pl.num_programs(0))
    def _():
        start(i + 1, 1 - slot)

    for k in range(2):
        _wait_rows(ys_hbm, buf.at[slot, k], sem.at[slot])
    rw = rw_ref[...]
    x2 = x1_ref[...] + buf[slot, 0] * rw[:, 0:1] + buf[slot, 1] * rw[:, 1:2]
    o_ref[...] = x2 * lax.rsqrt(jnp.mean(x2 * x2, axis=-1, keepdims=True) + NORM_EPS) * g_ref[...]


def _moe_combine(dest_flat, x1, rw, g, ys, tm=128):
    T, D = x1.shape
    grid_spec = pltpu.PrefetchScalarGridSpec(
        num_scalar_prefetch=1,
        grid=(T // tm,),
        in_specs=[
            pl.BlockSpec((tm, D), lambda i, d: (i, 0)),
            pl.BlockSpec((tm, LANE), lambda i, d: (i, 0)),
            pl.BlockSpec((1, D), lambda i, d: (0, 0)),
            pl.BlockSpec(memory_space=pl.ANY),
        ],
        out_specs=pl.BlockSpec((tm, D), lambda i, d: (i, 0)),
        scratch_shapes=[pltpu.VMEM((2, 2, tm, D), F32), pltpu.SemaphoreType.DMA((2,))],
    )
    return pl.pallas_call(
        _moe_combine_kernel,
        out_shape=jax.ShapeDtypeStruct((T, D), F32),
        grid_spec=grid_spec,
        compiler_params=_params(("arbitrary",)),
        name="moe_combine",
    )(dest_flat, x1, rw, g, ys)


def _slab_weights(w_in):
    D = w_in.shape[0]
    n_dn = 4 * DN_HEADS * HEAD_DIM
    n_small_a = 2 * DN_HEADS
    n_nsa = NSA_HEADS * HEAD_DIM + 6 * NSA_KV_GROUPS * HEAD_DIM
    c0 = n_dn + n_small_a
    c1 = c0 + n_nsa
    n_gate = 3 * NSA_HEADS
    pad = N_SLABS * LANE - (n_dn + n_nsa + n_small_a + n_gate)
    return jnp.concatenate(
        [w_in[:, :n_dn], w_in[:, c0:c1], w_in[:, n_dn:c0], w_in[:, c1:c1 + n_gate],
         jnp.zeros((D, pad), w_in.dtype)], axis=1).astype(BF16)


def _layer(x, norm_mix_g, w_in, conv_w, dt_bias, a_log, dn_norm_g, cmp_pos_k, cmp_w1_k, cmp_w2_k,
           cmp_pos_v, cmp_w1_v, cmp_w2_v, nsa_norm_g, w_out, norm_ffn_g, w_group, b_group,
           w_router, b_router, w_gate, w_up, w_down, norm_final_g):
    B, S, D = x.shape
    T = B * S
    x2d = x.reshape(T, D)

    P = _in_proj(x2d, norm_mix_g.reshape(1, D), _slab_weights(w_in))
    ydn = _deltanet(P, B, S, conv_w, dt_bias, a_log, dn_norm_g)

    kvc, kvct = _compress(P, B, S, jnp.stack([cmp_pos_k, cmp_pos_v]), jnp.stack([cmp_w1_k, cmp_w1_v]),
                          jnp.stack([cmp_w2_k, cmp_w2_v]))
    oct, selt, act = _nsa_select(P, kvc, kvct, B, S)
    ynsa = _nsa_attend(P, *_attend_operands(P, B, S), selt, _active_tiles(act, S), oct, B, S, nsa_norm_g)

    wr = jnp.zeros((D, LANE), F32).at[:, :N_GROUPS].set(w_group).at[:, N_GROUPS:N_GROUPS + N_EXPERTS].set(w_router)
    wr_hi = wr.astype(BF16)
    wr_lo = (wr - wr_hi.astype(F32)).astype(BF16)
    br = jnp.zeros((1, LANE), F32).at[0, :N_GROUPS].set(b_group).at[0, N_GROUPS:N_GROUPS + N_EXPERTS].set(b_router)
    x1, h2, ri, rw, cnt = _out_proj(ydn, ynsa, x2d, w_out.astype(BF16), norm_ffn_g.reshape(1, D),
                                    wr_hi, wr_lo, br)

    R = MOE_ROWS
    counts = cnt[0, :N_EXPERTS]
    padded = (counts + R - 1) // R * R
    pends = jnp.cumsum(padded)
    pstarts = pends - padded
    dest = pstarts[ri[:, 0:2]] + ri[:, 2:4]
    n_blocks = (2 * T + R - 1) // R + N_EXPERTS
    row_tok = jnp.zeros((n_blocks * R,), I32).at[dest.reshape(-1)].set(jnp.repeat(jnp.arange(T, dtype=I32), 2))
    block_start = jnp.arange(n_blocks, dtype=I32) * R
    block_expert = jnp.minimum(jnp.sum(block_start[:, None] >= pends[None, :], axis=1),
                               N_EXPERTS - 1).astype(I32)
    n_used = (pends[-1:] // R).astype(I32)

    ys = _moe_experts(h2, block_expert, n_used, row_tok, w_gate, w_up, w_down)
    out = _moe_combine(dest.T.reshape(-1).astype(I32), x1, rw, norm_final_g.reshape(1, D), ys)
    return out.reshape(B, S, D)


def kernel(x, norm_mix_g, w_in, conv_w, dt_bias, a_log, dn_norm_g, cmp_pos_k, cmp_w1_k, cmp_w2_k,
           cmp_pos_v, cmp_w1_v, cmp_w2_v, nsa_norm_g, w_out, norm_ffn_g, w_group, b_group,
           w_router, b_router, w_gate, w_up, w_down, norm_final_g):
    assert w_in.shape[0] == 1, "one layer"
    return _layer(x, norm_mix_g[0], w_in[0], conv_w[0], dt_bias[0], a_log[0], dn_norm_g[0],
                  cmp_pos_k[0], cmp_w1_k[0], cmp_w2_k[0], cmp_pos_v[0], cmp_w1_v[0], cmp_w2_v[0],
                  nsa_norm_g[0], w_out[0], norm_ffn_g[0], w_group[0], b_group[0], w_router[0],
                  b_router[0], w_gate[0], w_up[0], w_down[0], norm_final_g)
```

```python
import functools
import math

import numpy as np
import jax
import jax.numpy as jnp
from jax import lax
from jax.experimental import pallas as pl
from jax.experimental.pallas import tpu as pltpu

F32 = jnp.float32
BF16 = jnp.bfloat16
I32 = jnp.int32

LANE = 128
HEAD_DIM = 128
DN_HEADS = 8
NSA_HEADS = 8
NSA_KV_GROUPS = 2
NSA_HPG = NSA_HEADS // NSA_KV_GROUPS
CONV_WIDTH = 4
DN_CHUNK = 64
CMP_LEN = 32
CMP_STRIDE = 16
SEL_LEN = 64
SEL_COUNT = 16
WINDOW = 512
Q_BLOCK = 128
N_GROUPS = 8
EXPERTS_PER_GROUP = 8
N_EXPERTS = N_GROUPS * EXPERTS_PER_GROUP
NORM_EPS = 1e-6
NEG_INF = -1e30
SEL_FORCE = 1e9
PICKED = -3e38

SLAB_DN_Q, SLAB_DN_K, SLAB_DN_V, SLAB_DN_Z = 0, 8, 16, 24
SLAB_NSA_Q = 32
SLAB_K_CMP, SLAB_V_CMP, SLAB_K_SEL, SLAB_V_SEL, SLAB_K_WIN, SLAB_V_WIN = 40, 42, 44, 46, 48, 50
SLAB_SMALL = 52
N_SLABS = 54
LANE_DN_B, LANE_DN_A, LANE_GATE = 0, 8, 16

KEY_TILE = 256
MOE_ROWS = 256

VMEM_LIMIT = 56 * 1024 * 1024


def _params(sem, vmem=VMEM_LIMIT):
    return pltpu.CompilerParams(dimension_semantics=sem, vmem_limit_bytes=vmem)


def _dot(a, b):
    return jnp.dot(a, b, preferred_element_type=F32)


def _dot_nt(a, b):
    return lax.dot_general(a, b, (((1,), (1,)), ((), ())), preferred_element_type=F32)


def _dot_exact(a, b):
    return jnp.dot(a, b, preferred_element_type=F32, precision=lax.Precision.HIGHEST)


def _sigmoid(x):
    return 1.0 / (1.0 + jnp.exp(-x))


def _silu(x):
    return x * _sigmoid(x)


def _pick_lane(x, lane_ids, idx):
    return jnp.sum(jnp.where(lane_ids == idx, x, 0.0), axis=1, keepdims=True)


def _in_proj_kernel(x_ref, g_ref, w_ref, o_ref, h_scr):
    @pl.when(pl.program_id(1) == 0)
    def _():
        xf = x_ref[...]
        ms = jnp.mean(xf * xf, axis=-1, keepdims=True)
        h_scr[...] = (xf * lax.rsqrt(ms + NORM_EPS) * g_ref[...]).astype(BF16)

    acc = _dot(h_scr[...], w_ref[...])
    for k in range(o_ref.shape[0]):
        o_ref[k] = acc[:, k * LANE:(k + 1) * LANE]


def _in_proj(x2d, g, w_slabs, tm=1024, tn=768):
    T, D = x2d.shape
    NP = w_slabs.shape[1]
    ns = tn // LANE
    return pl.pallas_call(
        _in_proj_kernel,
        out_shape=jax.ShapeDtypeStruct((NP // LANE, T, LANE), F32),
        grid=(T // tm, NP // tn),
        in_specs=[
            pl.BlockSpec((tm, D), lambda i, j: (i, 0)),
            pl.BlockSpec((1, D), lambda i, j: (0, 0)),
            pl.BlockSpec((D, tn), lambda i, j: (0, j)),
        ],
        out_specs=pl.BlockSpec((ns, tm, LANE), lambda i, j: (j, i, 0)),
        scratch_shapes=[pltpu.VMEM((tm, D), BF16)],
        compiler_params=_params(("parallel", "arbitrary")),
        name="in_proj",
    )(x2d, g, w_slabs)


DN_HEADS_PER_ITER = 8


def _deltanet_kernel(p_ref, sm_ref, cw_ref, alog_ref, dtb_ref, ng_ref, o_ref,
                     ext_scr, s_scr, gc_scr, gl_scr, gct_scr, beta_scr):
    L = p_ref.shape[1]
    C = DN_CHUNK
    H = DN_HEADS

    @pl.when(pl.program_id(1) == 0)
    def _():
        ext_scr[:, 0:8, :] = jnp.zeros((3 * H, 8, LANE), F32)
        s_scr[...] = jnp.zeros_like(s_scr)

    sm = sm_ref[0]
    lane = lax.broadcasted_iota(I32, (1, LANE), 1)
    xa = sm + dtb_ref[...]
    softplus = jnp.maximum(xa, 0.0) + jnp.log1p(jnp.exp(-jnp.abs(xa)))
    ld = -jnp.exp(alog_ref[...]) * softplus
    ri = lax.broadcasted_iota(I32, (L, L), 0)
    ci = lax.broadcasted_iota(I32, (L, L), 1)
    same = (ri // C) == (ci // C)
    causal = same & (ci <= ri)
    gc_slab = _dot_exact(jnp.where(causal, 1.0, 0.0), ld)
    gc_scr[...] = gc_slab
    gct_scr[...] = gc_slab.T
    gl_scr[...] = _dot_exact(jnp.where(same, 1.0, 0.0), ld)
    beta_scr[...] = _sigmoid(sm)

    def conv_silu(slab, h):
        idx = slab + h
        ext_scr[idx, 8:, :] = p_ref[idx]
        w = cw_ref[idx]
        acc = ext_scr[idx, pl.ds(8 - (CONV_WIDTH - 1), L), :] * w[0:1]
        for j in range(1, CONV_WIDTH):
            acc = acc + ext_scr[idx, pl.ds(8 - (CONV_WIDTH - 1) + j, L), :] * w[j:j + 1]
        ext_scr[idx, 0:8, :] = ext_scr[idx, L:L + 8, :]
        return _silu(acc)

    def l2n(x):
        return x * lax.rsqrt(jnp.sum(x * x, axis=-1, keepdims=True) + NORM_EPS)

    def head_group(hg, carry):
        hs = [hg * DN_HEADS_PER_ITER + dh for dh in range(DN_HEADS_PER_ITER)]
        n = range(len(hs))
        qs = [l2n(conv_silu(SLAB_DN_Q, h)) * (HEAD_DIM ** -0.5) for h in hs]
        ks = [l2n(conv_silu(SLAB_DN_K, h)) for h in hs]
        vs = [conv_silu(SLAB_DN_V, h) for h in hs]
        betas = [_pick_lane(beta_scr[...], lane, LANE_DN_B + h) for h in hs]
        gcols = [_pick_lane(gc_scr[...], lane, LANE_DN_A + h) for h in hs]
        glasts = [_pick_lane(gl_scr[...], lane, LANE_DN_A + h) for h in hs]
        grows = [gct_scr[pl.ds(LANE_DN_A + h, 1), :] for h in hs]
        states = [s_scr[h] for h in hs]
        zs = [p_ref[SLAB_DN_Z + h] for h in hs]

        decays = [jnp.where(causal, jnp.exp(jnp.where(causal, gcols[i] - grows[i], 0.0)), 0.0) for i in n]
        egcs = [jnp.exp(gcols[i]) for i in n]
        kbs = [ks[i] * betas[i] for i in n]
        kts = [ks[i].astype(BF16) for i in n]
        lms = [jnp.where(ci < ri, _dot_nt(kbs[i].astype(BF16), kts[i]) * decays[i], 0.0) for i in n]
        tinvs = [jnp.where(ri == ci, 1.0, 0.0) - lms[i] for i in n]
        pws = lms
        for _ in range(int(math.log2(C)) - 1):
            pwbs = [pws[i].astype(BF16) for i in n]
            pws = [_dot(pwbs[i], pwbs[i]) for i in n]
            tinvs = [tinvs[i] + _dot(tinvs[i].astype(BF16), pws[i].astype(BF16)) for i in n]
        tbs = [tinvs[i].astype(BF16) for i in n]
        us = [_dot(tbs[i], (vs[i] * betas[i]).astype(BF16)) for i in n]
        ws = [_dot(tbs[i], (kbs[i] * egcs[i]).astype(BF16)) for i in n]
        qks = [_dot_nt(qs[i].astype(BF16), kts[i]) * decays[i] for i in n]
        q_decs = [(qs[i] * egcs[i]).astype(BF16) for i in n]
        k_dec_ts = [(ks[i] * jnp.exp(glasts[i] - gcols[i])).T for i in n]

        outs = [[] for _ in n]
        for c in range(L // C):
            sl = slice(c * C, (c + 1) * C)
            sbs = [states[i].astype(BF16) for i in n]
            vnbs = [(us[i][sl] - _dot(ws[i][sl].astype(BF16), sbs[i])).astype(BF16) for i in n]
            for i in n:
                outs[i].append(_dot(q_decs[i][sl], sbs[i]) + _dot(qks[i][sl, sl].astype(BF16), vnbs[i]))
            states = [states[i] * jnp.exp(glasts[i][c * C:c * C + 1, :])
                      + _dot(k_dec_ts[i][:, sl].astype(BF16), vnbs[i]) for i in n]

        for i, h in enumerate(hs):
            s_scr[h] = states[i]
            o = jnp.concatenate(outs[i], axis=0)
            on = o * lax.rsqrt(jnp.mean(o * o, axis=-1, keepdims=True) + NORM_EPS) * ng_ref[...]
            o_ref[h] = (on * _silu(zs[i])).astype(o_ref.dtype)
        return carry

    lax.fori_loop(0, H // DN_HEADS_PER_ITER, head_group, 0)


def _deltanet(P, B, S, conv_w, dt_bias, a_log, norm_g, L=256):
    T = B * S
    nL = S // L
    H = DN_HEADS
    cw = conv_w.reshape(CONV_WIDTH, 3 * H, LANE).transpose(1, 0, 2)
    alog = jnp.zeros((1, LANE), F32).at[0, LANE_DN_A:LANE_DN_A + H].set(a_log)
    dtb = jnp.zeros((1, LANE), F32).at[0, LANE_DN_A:LANE_DN_A + H].set(dt_bias)
    vec = pl.BlockSpec((1, LANE), lambda b, i: (0, 0))
    return pl.pallas_call(
        _deltanet_kernel,
        out_shape=jax.ShapeDtypeStruct((H, T, LANE), BF16),
        grid=(B, nL),
        in_specs=[
            pl.BlockSpec((4 * H, L, LANE), lambda b, i: (0, b * nL + i, 0)),
            pl.BlockSpec((1, L, LANE), lambda b, i: (SLAB_SMALL, b * nL + i, 0)),
            pl.BlockSpec((3 * H, CONV_WIDTH, LANE), lambda b, i: (0, 0, 0)),
            vec, vec, vec,
        ],
        out_specs=pl.BlockSpec((H, L, LANE), lambda b, i: (0, b * nL + i, 0)),
        scratch_shapes=[
            pltpu.VMEM((3 * H, L + 8, LANE), F32), pltpu.VMEM((H, HEAD_DIM, HEAD_DIM), F32),
            pltpu.VMEM((L, LANE), F32), pltpu.VMEM((L, LANE), F32), pltpu.VMEM((LANE, L), F32),
            pltpu.VMEM((L, LANE), F32),
        ],
        compiler_params=_params(("parallel", "arbitrary")),
        name="deltanet",
    )(P, P, cw, alog, dtb, norm_g.reshape(1, LANE))


def _gelu_tanh(x):
    return 0.5 * x * (1.0 + jnp.tanh(math.sqrt(2.0 / math.pi) * (x + 0.044715 * (x * x * x))))


def _compress_kernel(x_ref, pos_ref, w1_ref, w2_ref, o_ref, ot_ref):
    n = o_ref.shape[3]
    half = CMP_LEN // 2
    y1 = jnp.zeros((n, HEAD_DIM), F32)
    y2 = jnp.zeros((n, HEAD_DIM), F32)
    for l in range(half):
        xl = x_ref[0, pl.ds(l, n, stride=CMP_STRIDE), :]
        y1 = y1 + _dot((xl + pos_ref[0, l:l + 1, :]).astype(BF16), w1_ref[0, l].astype(BF16))
        y2 = y2 + _dot((xl + pos_ref[0, half + l:half + l + 1, :]).astype(BF16),
                       w1_ref[0, half + l].astype(BF16))
    hid = _gelu_tanh(y1 + pltpu.roll(y2, n - 1, 0))
    row = lax.broadcasted_iota(I32, (n, 1), 0)
    hb = jnp.where(row < n - 1, hid, 0.0).astype(BF16)
    w2 = w2_ref[0]
    o_ref[0, 0, 0] = _dot(hb, w2.astype(BF16)).astype(o_ref.dtype)
    ot_ref[0, 0, 0] = _dot_nt(w2.T.astype(BF16), hb).astype(ot_ref.dtype)


def _compress(P, B, S, pos, w1, w2):
    n = S // CMP_STRIDE
    G = NSA_KV_GROUPS
    return pl.pallas_call(
        _compress_kernel,
        out_shape=[jax.ShapeDtypeStruct((2, B, G, n, HEAD_DIM), BF16),
                   jax.ShapeDtypeStruct((2, B, G, HEAD_DIM, n), BF16)],
        grid=(2, B, G),
        in_specs=[
            pl.BlockSpec((1, S, LANE), lambda c, b, g: (SLAB_K_CMP + 2 * c + g, b, 0)),
            pl.BlockSpec((1, CMP_LEN, HEAD_DIM), lambda c, b, g: (c, 0, 0)),
            pl.BlockSpec((1, CMP_LEN, HEAD_DIM, HEAD_DIM), lambda c, b, g: (c, 0, 0, 0)),
            pl.BlockSpec((1, HEAD_DIM, HEAD_DIM), lambda c, b, g: (c, 0, 0)),
        ],
        out_specs=[pl.BlockSpec((1, 1, 1, n, HEAD_DIM), lambda c, b, g: (c, b, g, 0, 0)),
                   pl.BlockSpec((1, 1, 1, HEAD_DIM, n), lambda c, b, g: (c, b, g, 0, 0))],
        compiler_params=_params(("parallel", "parallel", "parallel")),
        name="compress",
    )(P, pos, w1, w2)


LOG2E = math.log2(math.e)
M_INIT = -1e29


def _q_transposed(q_ref):
    c1 = (HEAD_DIM ** -0.5) * LOG2E
    return jnp.concatenate([(q_ref[hh] * c1).T for hh in range(q_ref.shape[0])], axis=1).astype(BF16)


def _nsa_select_kernel(slopes_ref, q_ref, kc_ref, vct_ref, aggt_ref, grp_ref, oct_ref, selt_ref, act_ref):
    g = pl.program_id(1)
    blk = pl.program_id(2)
    Q, H = Q_BLOCK, NSA_HPG
    n = kc_ref.shape[3]
    ns = selt_ref.shape[3]
    st = _dot(kc_ref[0, 0, 0], _q_transposed(q_ref))
    vct = vct_ref[0, 0, 0]

    t = blk * Q + lax.broadcasted_iota(I32, (1, Q), 1)
    cpos = lax.broadcasted_iota(I32, (n, Q), 0) * CMP_STRIDE + (CMP_LEN - 1)
    valid = cpos <= t
    rel = (cpos - (blk * Q + Q - 1)).astype(F32)

    psum = jnp.zeros((n, Q), F32)
    for hh in range(H):
        sh = jnp.where(valid, st[:, hh * Q:(hh + 1) * Q] + rel * (slopes_ref[g * H + hh] * LOG2E), NEG_INF)
        e = jnp.where(valid, jnp.exp2(sh - jnp.max(sh, axis=0, keepdims=True)), 0.0)
        p = e * (1.0 / jnp.maximum(jnp.sum(e, axis=0, keepdims=True), 1e-30))
        oct_ref[0, 0, 0, hh * HEAD_DIM:(hh + 1) * HEAD_DIM, :] = _dot(vct, p.astype(BF16))
        psum = psum + p

    hi = psum.astype(BF16)
    lo = (psum - hi.astype(F32)).astype(BF16)
    imp = _dot(aggt_ref[...], hi) + _dot(aggt_ref[...], lo)

    sid = lax.broadcasted_iota(I32, (ns, Q), 0)
    sidf = sid.astype(F32)
    cur = t // SEL_LEN
    forced = (sid == 0) | (sid == cur) | (sid == cur - 1)
    valid_s = sid * SEL_LEN <= t
    work = jnp.where(forced, SEL_FORCE, jnp.where(valid_s, imp, NEG_INF))
    picked = jnp.zeros((ns, Q), F32)
    for _ in range(min(SEL_COUNT, ns)):
        m_w = jnp.max(work, axis=0, keepdims=True)
        first = jnp.min(jnp.where(work == m_w, sidf, float(ns)), axis=0, keepdims=True)
        hit = sidf == first
        picked = jnp.where(hit, 1.0, picked)
        work = jnp.where(hit, PICKED, work)
    selt = jnp.where(valid_s, picked, 0.0)
    selt_ref[0, 0, 0] = selt
    per_block = _dot_nt(jnp.ones((8, Q), BF16), selt.astype(BF16))
    tiles = _dot(per_block.astype(BF16), grp_ref[...])
    act_ref[0, 0, 0] = (tiles > 0.5).astype(I32)


def _cmp_to_sel_matrix(n_cmp_padded, n_sel):
    r = SEL_LEN // CMP_STRIDE
    c = CMP_LEN // CMP_STRIDE
    off = np.arange(n_cmp_padded)[:, None] - r * np.arange(n_sel)[None, :] + (c - 1)
    cnt = np.minimum(c - 1, off) - np.maximum(0, off - r + 1) + 1
    return np.clip(cnt, 0, None).astype(np.float32)


def _alibi_slopes(n_heads):
    return (2.0 ** (-8.0 * (np.arange(n_heads) + 1) / n_heads)).astype(np.float32)


def _nsa_select(P, kvc, kvct, B, S):
    G, Q, H = NSA_KV_GROUPS, Q_BLOCK, NSA_HPG
    nq = S // Q
    n = S // CMP_STRIDE
    ns = S // SEL_LEN
    aggt = jnp.asarray(_cmp_to_sel_matrix(n, ns).T, BF16)
    per_tile = KEY_TILE // SEL_LEN
    grp = jnp.asarray((np.arange(ns)[:, None] // per_tile) == np.arange(LANE)[None, :], BF16)
    slopes = jnp.asarray(_alibi_slopes(NSA_HEADS))
    per_blk = lambda b, g, i, s_: (b, g, i, 0, 0)
    grid_spec = pltpu.PrefetchScalarGridSpec(
        num_scalar_prefetch=1,
        grid=(B, G, nq),
        in_specs=[
            pl.BlockSpec((H, Q, LANE), lambda b, g, i, s_: (SLAB_NSA_Q // H + g, b * nq + i, 0)),
            pl.BlockSpec((1, 1, 1, n, HEAD_DIM), lambda b, g, i, s_: (0, b, g, 0, 0)),
            pl.BlockSpec((1, 1, 1, HEAD_DIM, n), lambda b, g, i, s_: (1, b, g, 0, 0)),
            pl.BlockSpec((ns, n), lambda b, g, i, s_: (0, 0)),
            pl.BlockSpec((ns, LANE), lambda b, g, i, s_: (0, 0)),
        ],
        out_specs=[
            pl.BlockSpec((1, 1, 1, H * HEAD_DIM, Q), per_blk),
            pl.BlockSpec((1, 1, 1, ns, Q), per_blk),
            pl.BlockSpec((1, 1, 1, 8, LANE), per_blk),
        ],
    )
    return pl.pallas_call(
        _nsa_select_kernel,
        out_shape=[
            jax.ShapeDtypeStruct((B, G, nq, H * HEAD_DIM, Q), F32),
            jax.ShapeDtypeStruct((B, G, nq, ns, Q), F32),
            jax.ShapeDtypeStruct((B, G, nq, 8, LANE), I32),
        ],
        grid_spec=grid_spec,
        compiler_params=_params(("parallel", "parallel", "parallel")),
        name="nsa_select",
    )(slopes, P, kvc, kvct, aggt, grp)


def _nsa_attend_kernel(slopes_ref, act_ref, q_ref, ks_ref, vst_ref, kw_ref, vwt_ref, selt_ref, oct_ref,
                       sm_ref, ng_ref, o_ref, m_scr, l_scr, acc_scr, sig_scr, *, tiles_max):
    b = pl.program_id(0)
    g = pl.program_id(1)
    blk = pl.program_id(2)
    nq = pl.num_programs(2)
    Q, H = Q_BLOCK, NSA_HPG
    per_tile = KEY_TILE // SEL_LEN

    def per_head(x):
        return jnp.concatenate([x] * H, axis=1)

    qt = _q_transposed(q_ref)
    t = blk * Q + lax.broadcasted_iota(I32, (1, Q), 1)
    t_last = blk * Q + Q - 1
    slope_row = jnp.concatenate(
        [jnp.full((1, Q), slopes_ref[g * H + hh] * LOG2E, F32) for hh in range(H)], axis=1)
    key_in_tile = lax.broadcasted_iota(I32, (KEY_TILE, Q), 0)
    diag = blk // (KEY_TILE // Q)

    def reset():
        m_scr[...] = jnp.full_like(m_scr, M_INIT)
        l_scr[...] = jnp.zeros_like(l_scr)
        acc_scr[...] = jnp.zeros_like(acc_scr)

    def update(k_ref, vt_ref, tiles):
        scores = []
        for j, mask_fn in tiles:
            start = pl.multiple_of(j * KEY_TILE, KEY_TILE)
            pos = start + key_in_tile
            s = _dot(k_ref[0, pl.ds(start, KEY_TILE), :], qt) + per_head((pos - t_last).astype(F32)) * slope_row
            scores.append(jnp.where(per_head(mask_fn(pos)), s, NEG_INF))
        m_old = m_scr[...]
        m_new = m_old
        for s in scores:
            m_new = jnp.maximum(m_new, jnp.max(s, axis=0, keepdims=True))
        alpha = jnp.exp2(m_old - m_new)
        l = alpha * l_scr[...]
        acc = alpha * acc_scr[...]
        for (j, _), s in zip(tiles, scores):
            p = jnp.exp2(s - m_new)
            l = l + jnp.sum(p, axis=0, keepdims=True)
            acc = acc + _dot(vt_ref[0, j], p.astype(BF16))
        l_scr[...] = l
        acc_scr[...] = acc
        m_scr[...] = m_new

    def finish():
        return acc_scr[...] * (1.0 / jnp.maximum(l_scr[...], 1e-30))

    def picked(rows):
        spread = jnp.concatenate(
            [jnp.broadcast_to(rows[c:c + 1], (SEL_LEN, Q)) for c in range(per_tile)], axis=0)
        return lambda pos: jnp.where(pos <= t, spread, 0.0) > 0.5

    reset()
    act_base = ((b * pl.num_programs(1) + g) * nq + blk) * tiles_max

    def pair_body(jp, carry):
        @pl.when((act_ref[act_base + 2 * jp] > 0) | (act_ref[act_base + 2 * jp + 1] > 0))
        def _():
            rows = selt_ref[0, 0, 0, pl.ds(pl.multiple_of(jp * 2 * per_tile, 2 * per_tile), 2 * per_tile), :]
            update(ks_ref, vst_ref, [(2 * jp, picked(rows[0:per_tile])),
                                     (2 * jp + 1, picked(rows[per_tile:2 * per_tile]))])
        return carry

    lax.fori_loop(0, diag // 2 + 1, pair_body, 0)
    o_sel = finish()

    def in_window(exists):
        def mask(pos):
            dist = jnp.where(pos <= t, t - pos, WINDOW)
            return jnp.where(exists, dist, WINDOW) < WINDOW
        return mask

    reset()
    update(kw_ref, vwt_ref, [(jnp.maximum(diag - back, 0), in_window(diag - back >= 0))
                             for back in range(WINDOW // KEY_TILE, -1, -1)])
    o_win = finish()

    sig_scr[...] = _sigmoid(sm_ref[0].T)
    for hh in range(H):
        cols = slice(hh * Q, (hh + 1) * Q)
        gl = LANE_GATE + (g * H + hh) * 3
        o = (sig_scr[pl.ds(gl, 1), :] * oct_ref[0, 0, 0, hh * HEAD_DIM:(hh + 1) * HEAD_DIM, :]
             + sig_scr[pl.ds(gl + 1, 1), :] * o_sel[:, cols] + sig_scr[pl.ds(gl + 2, 1), :] * o_win[:, cols])
        on = o * lax.rsqrt(jnp.mean(o * o, axis=0, keepdims=True) + NORM_EPS) * ng_ref[...]
        o_ref[:, hh * HEAD_DIM:(hh + 1) * HEAD_DIM] = on.T.astype(o_ref.dtype)


def _nsa_attend(P, k_bf16, vt_tiles, selt, act_flat, oct, B, S, norm_g):
    G, Q, H = NSA_KV_GROUPS, Q_BLOCK, NSA_HPG
    nq = S // Q
    ns = S // SEL_LEN
    nt = S // KEY_TILE
    T = B * S
    slopes = jnp.asarray(_alibi_slopes(NSA_HEADS))
    ng = jnp.broadcast_to(norm_g[:, None], (HEAD_DIM, Q))
    per_blk = lambda b, g, i, *_: (b, g, i, 0, 0)

    def k_spec(base):
        return pl.BlockSpec((1, S, LANE), lambda b, g, i, *_: (base + g, b, 0))

    def vt_spec(base):
        return pl.BlockSpec((1, nt, HEAD_DIM, KEY_TILE), lambda b, g, i, *_: (base + g, b, 0, 0))

    grid_spec = pltpu.PrefetchScalarGridSpec(
        num_scalar_prefetch=2,
        grid=(B, G, nq),
        in_specs=[
            pl.BlockSpec((H, Q, LANE), lambda b, g, i, *_: (SLAB_NSA_Q // H + g, b * nq + i, 0)),
            k_spec(0), vt_spec(0), k_spec(G), vt_spec(G),
            pl.BlockSpec((1, 1, 1, ns, Q), per_blk),
            pl.BlockSpec((1, 1, 1, H * HEAD_DIM, Q), per_blk),
            pl.BlockSpec((1, Q, LANE), lambda b, g, i, *_: (SLAB_SMALL, b * nq + i, 0)),
            pl.BlockSpec((HEAD_DIM, Q), lambda b, g, i, *_: (0, 0)),
        ],
        out_specs=pl.BlockSpec((Q, H * HEAD_DIM), lambda b, g, i, *_: (b * nq + i, g)),
        scratch_shapes=[
            pltpu.VMEM((1, H * Q), F32), pltpu.VMEM((1, H * Q), F32),
            pltpu.VMEM((HEAD_DIM, H * Q), F32), pltpu.VMEM((LANE, Q), F32),
        ],
    )
    return pl.pallas_call(
        functools.partial(_nsa_attend_kernel, tiles_max=nt),
        out_shape=jax.ShapeDtypeStruct((T, NSA_HEADS * HEAD_DIM), BF16),
        grid_spec=grid_spec,
        compiler_params=_params(("parallel", "parallel", "arbitrary")),
        name="nsa_attend",
    )(slopes, act_flat, P, k_bf16, vt_tiles, k_bf16, vt_tiles, selt, oct, P, ng)


def _kv_prep_kernel(k_ref, v_ref, ko_ref, vo_ref):
    ko_ref[0] = k_ref[0].astype(ko_ref.dtype)
    for c in range(vo_ref.shape[1]):
        vo_ref[0, c] = v_ref[0, c * KEY_TILE:(c + 1) * KEY_TILE, :].T.astype(vo_ref.dtype)


def _attend_operands(P, B, S, tm=1024):
    G = NSA_KV_GROUPS
    T = B * S
    tm = min(tm, S)
    per_step = tm // KEY_TILE

    def slab(first):
        return lambda s, i: (first + (s // G) * (SLAB_K_WIN - SLAB_K_SEL) + s % G, i, 0)

    return pl.pallas_call(
        _kv_prep_kernel,
        out_shape=[jax.ShapeDtypeStruct((2 * G, T, LANE), BF16),
                   jax.ShapeDtypeStruct((2 * G, T // KEY_TILE, HEAD_DIM, KEY_TILE), BF16)],
        grid=(2 * G, T // tm),
        in_specs=[pl.BlockSpec((1, tm, LANE), slab(SLAB_K_SEL)), pl.BlockSpec((1, tm, LANE), slab(SLAB_V_SEL))],
        out_specs=[pl.BlockSpec((1, tm, LANE), lambda s, i: (s, i, 0)),
                   pl.BlockSpec((1, per_step, HEAD_DIM, KEY_TILE), lambda s, i: (s, i, 0, 0))],
        compiler_params=_params(("parallel", "parallel")),
        name="kv_prep",
    )(P, P)


def _active_tiles(act, S):
    return act[:, :, :, 0, :S // KEY_TILE].reshape(-1)


def _out_proj_kernel(ydn_ref, ynsa_ref, x_ref, wo_ref, g_ref, wrh_ref, wrl_ref, br_ref,
                     x1_ref, h2_ref, ri_ref, rw_ref, cnt_ref, carry_scr):
    tm = x_ref.shape[0]
    half = ynsa_ref.shape[1]

    @pl.when(pl.program_id(0) == 0)
    def _():
        carry_scr[...] = jnp.zeros_like(carry_scr)

    ydn = jnp.concatenate([ydn_ref[hh] for hh in range(ydn_ref.shape[0])], axis=1)
    x1 = x_ref[...] + _dot(ydn, wo_ref[0:half, :]) + _dot(ynsa_ref[...], wo_ref[half:, :])
    x1_ref[...] = x1
    h2 = x1 * lax.rsqrt(jnp.mean(x1 * x1, axis=-1, keepdims=True) + NORM_EPS) * g_ref[...]
    h2_ref[...] = h2

    hi = h2.astype(BF16)
    lo = (h2 - hi.astype(F32)).astype(BF16)
    lg = _dot(hi, wrh_ref[...]) + _dot(hi, wrl_ref[...]) + _dot(lo, wrh_ref[...])
    lane = lax.broadcasted_iota(I32, (1, LANE), 1)
    lanef = lane.astype(F32)
    biased = lg + br_ref[...]

    is_grp = lane < N_GROUPS
    gmax = jnp.max(jnp.where(is_grp, lg, NEG_INF), axis=-1, keepdims=True)
    eg = jnp.where(is_grp, jnp.exp(lg - gmax), 0.0)
    pg = eg / jnp.sum(eg, axis=-1, keepdims=True)
    gb = jnp.where(is_grp, biased, NEG_INF)
    g_sel = jnp.min(jnp.where(gb == jnp.max(gb, axis=-1, keepdims=True), lanef, float(LANE)),
                    axis=-1, keepdims=True)
    p_sel = _pick_lane(pg, lanef, g_sel)

    lo_lane = N_GROUPS + g_sel * EXPERTS_PER_GROUP
    in_grp = (lanef >= lo_lane) & (lanef < lo_lane + EXPERTS_PER_GROUP)
    v1 = jnp.where(in_grp, biased, NEG_INF)
    i1 = jnp.min(jnp.where(v1 == jnp.max(v1, axis=-1, keepdims=True), lanef, float(LANE)),
                 axis=-1, keepdims=True)
    v2 = jnp.where(lanef == i1, NEG_INF * 2, v1)
    i2 = jnp.min(jnp.where(v2 == jnp.max(v2, axis=-1, keepdims=True), lanef, float(LANE)),
                 axis=-1, keepdims=True)
    el1 = _pick_lane(lg, lanef, i1)
    el2 = _pick_lane(lg, lanef, i2)
    mx = jnp.maximum(el1, el2)
    e1 = jnp.exp(el1 - mx)
    e2 = jnp.exp(el2 - mx)
    rw_ref[...] = jnp.where(lane == 0, p_sel * (e1 / (e1 + e2)),
                            jnp.where(lane == 1, p_sel * (e2 / (e1 + e2)), 0.0))

    ex1 = i1 - N_GROUPS
    ex2 = i2 - N_GROUPS
    oh1 = lanef == ex1
    oh2 = lanef == ex2
    both = jnp.where(oh1 | oh2, 1.0, 0.0)
    ri = lax.broadcasted_iota(I32, (tm, tm), 0)
    ci = lax.broadcasted_iota(I32, (tm, tm), 1)
    before = _dot((ci < ri).astype(BF16), both.astype(BF16)) + carry_scr[...]
    r1 = jnp.sum(jnp.where(oh1, before, 0.0), axis=-1, keepdims=True)
    r2 = jnp.sum(jnp.where(oh2, before, 0.0), axis=-1, keepdims=True)
    packed = jnp.where(lane == 0, ex1, jnp.where(lane == 1, ex2,
                       jnp.where(lane == 2, r1, jnp.where(lane == 3, r2, 0.0))))
    ri_ref[...] = packed.astype(I32)
    total = carry_scr[...] + jnp.sum(both, axis=0, keepdims=True)
    carry_scr[...] = total
    cnt_ref[...] = jnp.broadcast_to(total, cnt_ref.shape).astype(I32)


def _out_proj(ydn, ynsa, x2d, wo, g, wr_hi, wr_lo, br, tm=256):
    T, D = x2d.shape
    half = ynsa.shape[1]
    row = lambda i: (i, 0)
    fixed = lambda i: (0, 0)
    return pl.pallas_call(
        _out_proj_kernel,
        out_shape=[
            jax.ShapeDtypeStruct((T, D), F32), jax.ShapeDtypeStruct((T, D), F32),
            jax.ShapeDtypeStruct((T, LANE), I32), jax.ShapeDtypeStruct((T, LANE), F32),
            jax.ShapeDtypeStruct((8, LANE), I32),
        ],
        grid=(T // tm,),
        in_specs=[
            pl.BlockSpec((ydn.shape[0], tm, LANE), lambda i: (0, i, 0)),
            pl.BlockSpec((tm, half), row), pl.BlockSpec((tm, D), row),
            pl.BlockSpec((D, D), fixed), pl.BlockSpec((1, D), fixed),
            pl.BlockSpec((D, LANE), fixed), pl.BlockSpec((D, LANE), fixed), pl.BlockSpec((1, LANE), fixed),
        ],
        out_specs=[
            pl.BlockSpec((tm, D), row), pl.BlockSpec((tm, D), row),
            pl.BlockSpec((tm, LANE), row), pl.BlockSpec((tm, LANE), row), pl.BlockSpec((8, LANE), fixed),
        ],
        scratch_shapes=[pltpu.VMEM((1, LANE), F32)],
        compiler_params=_params(("arbitrary",)),
        name="out_proj_router",
    )(ydn, ynsa, x2d, wo, g, wr_hi, wr_lo, br)


def _row_copy(src_hbm, row, dst, slot, sem):
    return pltpu.make_async_copy(src_hbm.at[pl.ds(row, 1)], dst.at[pl.ds(slot, 1)], sem)


GATHER_UNROLL = 8


def _gather_rows(src_hbm, idx_ref, base, dst, sem, n_rows):
    def issue(r, c):
        _row_copy(src_hbm, idx_ref[base + r], dst, r, sem).start()
        return c
    lax.fori_loop(0, n_rows, issue, 0, unroll=GATHER_UNROLL)


def _wait_rows(src_hbm, dst, sem):
    pltpu.make_async_copy(src_hbm.at[pl.ds(0, dst.shape[0])], dst, sem).wait()


def _moe_experts_kernel(be_ref, nu_ref, rt_ref, h_hbm, wg_ref, wu_ref, wd_ref, y_ref,
                        xs_scr, wg_scr, wu_scr, wd_scr, sem):
    i = pl.program_id(0)
    R = xs_scr.shape[1]
    slot = i % 2

    @pl.when((i == 0) & (nu_ref[0] > 0))
    def _():
        _gather_rows(h_hbm, rt_ref, 0, xs_scr.at[0], sem.at[0], R)

    @pl.when(i + 1 < nu_ref[0])
    def _():
        _gather_rows(h_hbm, rt_ref, (i + 1) * R, xs_scr.at[1 - slot], sem.at[1 - slot], R)

    @pl.when(i < nu_ref[0])
    def _():
        @pl.when((i == 0) | (be_ref[jnp.maximum(i - 1, 0)] != be_ref[i]))
        def _():
            wg_scr[...] = wg_ref[0].astype(BF16)
            wu_scr[...] = wu_ref[0].astype(BF16)
            wd_scr[...] = wd_ref[0].astype(BF16)

        _wait_rows(h_hbm, xs_scr.at[slot], sem.at[slot])
        xb = xs_scr[slot].astype(BF16)
        hid = _silu(_dot(xb, wg_scr[...])) * _dot(xb, wu_scr[...])
        y_ref[...] = _dot(hid.astype(BF16), wd_scr[...])

    @pl.when(i >= nu_ref[0])
    def _():
        y_ref[...] = jnp.zeros_like(y_ref)


def _moe_experts(h2, block_expert, n_used, row_tok, w_gate, w_up, w_down):
    T, D = h2.shape
    R = MOE_ROWS
    n_blocks = block_expert.shape[0]
    DE = w_gate.shape[2]
    grid_spec = pltpu.PrefetchScalarGridSpec(
        num_scalar_prefetch=3,
        grid=(n_blocks,),
        in_specs=[
            pl.BlockSpec(memory_space=pl.ANY),
            pl.BlockSpec((1, D, DE), lambda i, be, nu, rt: (be[i], 0, 0)),
            pl.BlockSpec((1, D, DE), lambda i, be, nu, rt: (be[i], 0, 0)),
            pl.BlockSpec((1, DE, D), lambda i, be, nu, rt: (be[i], 0, 0)),
        ],
        out_specs=pl.BlockSpec((R, D), lambda i, be, nu, rt: (i, 0)),
        scratch_shapes=[
            pltpu.VMEM((2, R, D), F32), pltpu.VMEM((D, DE), BF16), pltpu.VMEM((D, DE), BF16),
            pltpu.VMEM((DE, D), BF16), pltpu.SemaphoreType.DMA((2,)),
        ],
    )
    return pl.pallas_call(
        _moe_experts_kernel,
        out_shape=jax.ShapeDtypeStruct((n_blocks * R, D), F32),
        grid_spec=grid_spec,
        compiler_params=_params(("arbitrary",)),
        name="moe_experts",
    )(block_expert, n_used, row_tok, h2, w_gate, w_up, w_down)


def _moe_combine_kernel(dest_ref, x1_ref, rw_ref, g_ref, ys_hbm, o_ref, buf, sem):
    i = pl.program_id(0)
    tm = x1_ref.shape[0]
    T = tm * pl.num_programs(0)
    slot = i % 2

    def start(step, s):
        for k in range(2):
            _gather_rows(ys_hbm, dest_ref, k * T + step * tm, buf.at[s, k], sem.at[s], tm)

    @pl.when(i == 0)
    def _():
        start(0, 0)

    @pl.when(i + 1 < pl.num_programs(0))
    def _():
        start(i + 1, 1 - slot)

    for k in range(2):
        _wait_rows(ys_hbm, buf.at[slot, k], sem.at[slot])
    rw = rw_ref[...]
    x2 = x1_ref[...] + buf[slot, 0] * rw[:, 0:1] + buf[slot, 1] * rw[:, 1:2]
    o_ref[...] = x2 * lax.rsqrt(jnp.mean(x2 * x2, axis=-1, keepdims=True) + NORM_EPS) * g_ref[...]


def _moe_combine(dest_flat, x1, rw, g, ys, tm=128):
    T, D = x1.shape
    grid_spec = pltpu.PrefetchScalarGridSpec(
        num_scalar_prefetch=1,
        grid=(T // tm,),
        in_specs=[
            pl.BlockSpec((tm, D), lambda i, d: (i, 0)),
            pl.BlockSpec((tm, LANE), lambda i, d: (i, 0)),
            pl.BlockSpec((1, D), lambda i, d: (0, 0)),
            pl.BlockSpec(memory_space=pl.ANY),
        ],
        out_specs=pl.BlockSpec((tm, D), lambda i, d: (i, 0)),
        scratch_shapes=[pltpu.VMEM((2, 2, tm, D), F32), pltpu.SemaphoreType.DMA((2,))],
    )
    return pl.pallas_call(
        _moe_combine_kernel,
        out_shape=jax.ShapeDtypeStruct((T, D), F32),
        grid_spec=grid_spec,
        compiler_params=_params(("arbitrary",)),
        name="moe_combine",
    )(dest_flat, x1, rw, g, ys)


def _slab_weights(w_in):
    D = w_in.shape[0]
    n_dn = 4 * DN_HEADS * HEAD_DIM
    n_small_a = 2 * DN_HEADS
    n_nsa = NSA_HEADS * HEAD_DIM + 6 * NSA_KV_GROUPS * HEAD_DIM
    c0 = n_dn + n_small_a
    c1 = c0 + n_nsa
    n_gate = 3 * NSA_HEADS
    pad = N_SLABS * LANE - (n_dn + n_nsa + n_small_a + n_gate)
    return jnp.concatenate(
        [w_in[:, :n_dn], w_in[:, c0:c1], w_in[:, n_dn:c0], w_in[:, c1:c1 + n_gate],
         jnp.zeros((D, pad), w_in.dtype)], axis=1).astype(BF16)


def _layer(x, norm_mix_g, w_in, conv_w, dt_bias, a_log, dn_norm_g, cmp_pos_k, cmp_w1_k, cmp_w2_k,
           cmp_pos_v, cmp_w1_v, cmp_w2_v, nsa_norm_g, w_out, norm_ffn_g, w_group, b_group,
           w_router, b_router, w_gate, w_up, w_down, norm_final_g):
    B, S, D = x.shape
    T = B * S
    x2d = x.reshape(T, D)

    P = _in_proj(x2d, norm_mix_g.reshape(1, D), _slab_weights(w_in))
    ydn = _deltanet(P, B, S, conv_w, dt_bias, a_log, dn_norm_g)

    kvc, kvct = _compress(P, B, S, jnp.stack([cmp_pos_k, cmp_pos_v]), jnp.stack([cmp_w1_k, cmp_w1_v]),
                          jnp.stack([cmp_w2_k, cmp_w2_v]))
    oct, selt, act = _nsa_select(P, kvc, kvct, B, S)
    ynsa = _nsa_attend(P, *_attend_operands(P, B, S), selt, _active_tiles(act, S), oct, B, S, nsa_norm_g)

    wr = jnp.zeros((D, LANE), F32).at[:, :N_GROUPS].set(w_group).at[:, N_GROUPS:N_GROUPS + N_EXPERTS].set(w_router)
    wr_hi = wr.astype(BF16)
    wr_lo = (wr - wr_hi.astype(F32)).astype(BF16)
    br = jnp.zeros((1, LANE), F32).at[0, :N_GROUPS].set(b_group).at[0, N_GROUPS:N_GROUPS + N_EXPERTS].set(b_router)
    x1, h2, ri, rw, cnt = _out_proj(ydn, ynsa, x2d, w_out.astype(BF16), norm_ffn_g.reshape(1, D),
                                    wr_hi, wr_lo, br)

    R = MOE_ROWS
    counts = cnt[0, :N_EXPERTS]
    padded = (counts + R - 1) // R * R
    pends = jnp.cumsum(padded)
    pstarts = pends - padded
    first_row = jnp.sum(jnp.where(ri[:, 0:2, None] == jnp.arange(N_EXPERTS, dtype=I32), pstarts, 0), axis=-1)
    dest = first_row + ri[:, 2:4]
    n_blocks = (2 * T + R - 1) // R + N_EXPERTS
    row_tok = jnp.zeros((n_blocks * R,), I32).at[dest.reshape(-1)].set(jnp.repeat(jnp.arange(T, dtype=I32), 2))
    block_start = jnp.arange(n_blocks, dtype=I32) * R
    block_expert = jnp.minimum(jnp.sum(block_start[:, None] >= pends[None, :], axis=1),
                               N_EXPERTS - 1).astype(I32)
    n_used = (pends[-1:] // R).astype(I32)

    ys = _moe_experts(h2, block_expert, n_used, row_tok, w_gate, w_up, w_down)
    out = _moe_combine(dest.T.reshape(-1).astype(I32), x1, rw, norm_final_g.reshape(1, D), ys)
    return out.reshape(B, S, D)


def kernel(x, norm_mix_g, w_in, conv_w, dt_bias, a_log, dn_norm_g, cmp_pos_k, cmp_w1_k, cmp_w2_k,
           cmp_pos_v, cmp_w1_v, cmp_w2_v, nsa_norm_g, w_out, norm_ffn_g, w_group, b_group,
           w_router, b_router, w_gate, w_up, w_down, norm_final_g):
    assert w_in.shape[0] == 1, "one layer"
    return _layer(x, norm_mix_g[0], w_in[0], conv_w[0], dt_bias[0], a_log[0], dn_norm_g[0],
                  cmp_pos_k[0], cmp_w1_k[0], cmp_w2_k[0], cmp_pos_v[0], cmp_w1_v[0], cmp_w2_v[0],
                  nsa_norm_g[0], w_out[0], norm_ffn_g[0], w_group[0], b_group[0], w_router[0],
                  b_router[0], w_gate[0], w_up[0], w_down[0], norm_final_g)
```

```python
import functools
import math

import numpy as np
import jax
import jax.numpy as jnp
from jax import lax
from jax.experimental import pallas as pl
from jax.experimental.pallas import tpu as pltpu

F32 = jnp.float32
BF16 = jnp.bfloat16
I32 = jnp.int32

LANE = 128
HEAD_DIM = 128
DN_HEADS = 8
NSA_HEADS = 8
NSA_KV_GROUPS = 2
NSA_HPG = NSA_HEADS // NSA_KV_GROUPS
CONV_WIDTH = 4
DN_CHUNK = 64
CMP_LEN = 32
CMP_STRIDE = 16
SEL_LEN = 64
SEL_COUNT = 16
WINDOW = 512
Q_BLOCK = 128
N_GROUPS = 8
EXPERTS_PER_GROUP = 8
N_EXPERTS = N_GROUPS * EXPERTS_PER_GROUP
NORM_EPS = 1e-6
NEG_INF = -1e30
SEL_FORCE = 1e9
PICKED = -3e38

SLAB_DN_Q, SLAB_DN_K, SLAB_DN_V, SLAB_DN_Z = 0, 8, 16, 24
SLAB_NSA_Q = 32
SLAB_K_CMP, SLAB_V_CMP, SLAB_K_SEL, SLAB_V_SEL, SLAB_K_WIN, SLAB_V_WIN = 40, 42, 44, 46, 48, 50
SLAB_SMALL = 52
N_SLABS = 54
LANE_DN_B, LANE_DN_A, LANE_GATE = 0, 8, 16

KEY_TILE = 256
MOE_ROWS = 256

VMEM_LIMIT = 56 * 1024 * 1024


def _params(sem, vmem=VMEM_LIMIT):
    return pltpu.CompilerParams(dimension_semantics=sem, vmem_limit_bytes=vmem)


def _dot(a, b):
    return jnp.dot(a, b, preferred_element_type=F32)


def _dot_nt(a, b):
    return lax.dot_general(a, b, (((1,), (1,)), ((), ())), preferred_element_type=F32)


def _dot_exact(a, b):
    return jnp.dot(a, b, preferred_element_type=F32, precision=lax.Precision.HIGHEST)


def _sigmoid(x):
    return 1.0 / (1.0 + jnp.exp(-x))


def _silu(x):
    return x * _sigmoid(x)


def _pick_lane(x, lane_ids, idx):
    return jnp.sum(jnp.where(lane_ids == idx, x, 0.0), axis=1, keepdims=True)


def _in_proj_kernel(x_ref, g_ref, w_ref, o_ref, h_scr):
    @pl.when(pl.program_id(1) == 0)
    def _():
        xf = x_ref[...]
        ms = jnp.mean(xf * xf, axis=-1, keepdims=True)
        h_scr[...] = (xf * lax.rsqrt(ms + NORM_EPS) * g_ref[...]).astype(BF16)

    acc = _dot(h_scr[...], w_ref[...])
    for k in range(o_ref.shape[0]):
        o_ref[k] = acc[:, k * LANE:(k + 1) * LANE]


def _in_proj(x2d, g, w_slabs, tm=1024, tn=768):
    T, D = x2d.shape
    NP = w_slabs.shape[1]
    ns = tn // LANE
    return pl.pallas_call(
        _in_proj_kernel,
        out_shape=jax.ShapeDtypeStruct((NP // LANE, T, LANE), F32),
        grid=(T // tm, NP // tn),
        in_specs=[
            pl.BlockSpec((tm, D), lambda i, j: (i, 0)),
            pl.BlockSpec((1, D), lambda i, j: (0, 0)),
            pl.BlockSpec((D, tn), lambda i, j: (0, j)),
        ],
        out_specs=pl.BlockSpec((ns, tm, LANE), lambda i, j: (j, i, 0)),
        scratch_shapes=[pltpu.VMEM((tm, D), BF16)],
        compiler_params=_params(("parallel", "arbitrary")),
        name="in_proj",
    )(x2d, g, w_slabs)


DN_HEADS_PER_ITER = 8


def _deltanet_kernel(p_ref, sm_ref, cw_ref, alog_ref, dtb_ref, ng_ref, o_ref,
                     ext_scr, s_scr, gc_scr, gl_scr, gct_scr, beta_scr):
    L = p_ref.shape[1]
    C = DN_CHUNK
    H = DN_HEADS

    @pl.when(pl.program_id(1) == 0)
    def _():
        ext_scr[:, 0:8, :] = jnp.zeros((3 * H, 8, LANE), F32)
        s_scr[...] = jnp.zeros_like(s_scr)

    sm = sm_ref[0]
    lane = lax.broadcasted_iota(I32, (1, LANE), 1)
    xa = sm + dtb_ref[...]
    softplus = jnp.maximum(xa, 0.0) + jnp.log1p(jnp.exp(-jnp.abs(xa)))
    ld = -jnp.exp(alog_ref[...]) * softplus
    ri = lax.broadcasted_iota(I32, (L, L), 0)
    ci = lax.broadcasted_iota(I32, (L, L), 1)
    same = (ri // C) == (ci // C)
    causal = same & (ci <= ri)
    gc_slab = _dot_exact(jnp.where(causal, 1.0, 0.0), ld)
    gc_scr[...] = gc_slab
    gct_scr[...] = gc_slab.T
    gl_scr[...] = _dot_exact(jnp.where(same, 1.0, 0.0), ld)
    beta_scr[...] = _sigmoid(sm)

    def conv_silu(slab, h):
        idx = slab + h
        ext_scr[idx, 8:, :] = p_ref[idx]
        w = cw_ref[idx]
        acc = ext_scr[idx, pl.ds(8 - (CONV_WIDTH - 1), L), :] * w[0:1]
        for j in range(1, CONV_WIDTH):
            acc = acc + ext_scr[idx, pl.ds(8 - (CONV_WIDTH - 1) + j, L), :] * w[j:j + 1]
        ext_scr[idx, 0:8, :] = ext_scr[idx, L:L + 8, :]
        return _silu(acc)

    def l2n(x):
        return x * lax.rsqrt(jnp.sum(x * x, axis=-1, keepdims=True) + NORM_EPS)

    def head_group(hg, carry):
        hs = [hg * DN_HEADS_PER_ITER + dh for dh in range(DN_HEADS_PER_ITER)]
        n = range(len(hs))
        qs = [l2n(conv_silu(SLAB_DN_Q, h)) * (HEAD_DIM ** -0.5) for h in hs]
        ks = [l2n(conv_silu(SLAB_DN_K, h)) for h in hs]
        vs = [conv_silu(SLAB_DN_V, h) for h in hs]
        betas = [_pick_lane(beta_scr[...], lane, LANE_DN_B + h) for h in hs]
        gcols = [_pick_lane(gc_scr[...], lane, LANE_DN_A + h) for h in hs]
        glasts = [_pick_lane(gl_scr[...], lane, LANE_DN_A + h) for h in hs]
        grows = [gct_scr[pl.ds(LANE_DN_A + h, 1), :] for h in hs]
        states = [s_scr[h] for h in hs]
        zs = [p_ref[SLAB_DN_Z + h] for h in hs]

        decays = [jnp.where(causal, jnp.exp(jnp.where(causal, gcols[i] - grows[i], 0.0)), 0.0) for i in n]
        egcs = [jnp.exp(gcols[i]) for i in n]
        kbs = [ks[i] * betas[i] for i in n]
        kts = [ks[i].astype(BF16) for i in n]
        lms = [jnp.where(ci < ri, _dot_nt(kbs[i].astype(BF16), kts[i]) * decays[i], 0.0) for i in n]
        tinvs = [jnp.where(ri == ci, 1.0, 0.0) - lms[i] for i in n]
        pws = lms
        for _ in range(int(math.log2(C)) - 1):
            pwbs = [pws[i].astype(BF16) for i in n]
            pws = [_dot(pwbs[i], pwbs[i]) for i in n]
            tinvs = [tinvs[i] + _dot(tinvs[i].astype(BF16), pws[i].astype(BF16)) for i in n]
        tbs = [tinvs[i].astype(BF16) for i in n]
        us = [_dot(tbs[i], (vs[i] * betas[i]).astype(BF16)) for i in n]
        ws = [_dot(tbs[i], (kbs[i] * egcs[i]).astype(BF16)) for i in n]
        qks = [_dot_nt(qs[i].astype(BF16), kts[i]) * decays[i] for i in n]
        q_decs = [(qs[i] * egcs[i]).astype(BF16) for i in n]
        k_dec_ts = [(ks[i] * jnp.exp(glasts[i] - gcols[i])).T for i in n]

        outs = [[] for _ in n]
        for c in range(L // C):
            sl = slice(c * C, (c + 1) * C)
            sbs = [states[i].astype(BF16) for i in n]
            vnbs = [(us[i][sl] - _dot(ws[i][sl].astype(BF16), sbs[i])).astype(BF16) for i in n]
            for i in n:
                outs[i].append(_dot(q_decs[i][sl], sbs[i]) + _dot(qks[i][sl, sl].astype(BF16), vnbs[i]))
            states = [states[i] * jnp.exp(glasts[i][c * C:c * C + 1, :])
                      + _dot(k_dec_ts[i][:, sl].astype(BF16), vnbs[i]) for i in n]

        for i, h in enumerate(hs):
            s_scr[h] = states[i]
            o = jnp.concatenate(outs[i], axis=0)
            on = o * lax.rsqrt(jnp.mean(o * o, axis=-1, keepdims=True) + NORM_EPS) * ng_ref[...]
            o_ref[h] = (on * _silu(zs[i])).astype(o_ref.dtype)
        return carry

    lax.fori_loop(0, H // DN_HEADS_PER_ITER, head_group, 0)


def _deltanet(P, B, S, conv_w, dt_bias, a_log, norm_g, L=256):
    T = B * S
    nL = S // L
    H = DN_HEADS
    cw = conv_w.reshape(CONV_WIDTH, 3 * H, LANE).transpose(1, 0, 2)
    alog = jnp.zeros((1, LANE), F32).at[0, LANE_DN_A:LANE_DN_A + H].set(a_log)
    dtb = jnp.zeros((1, LANE), F32).at[0, LANE_DN_A:LANE_DN_A + H].set(dt_bias)
    vec = pl.BlockSpec((1, LANE), lambda b, i: (0, 0))
    return pl.pallas_call(
        _deltanet_kernel,
        out_shape=jax.ShapeDtypeStruct((H, T, LANE), BF16),
        grid=(B, nL),
        in_specs=[
            pl.BlockSpec((4 * H, L, LANE), lambda b, i: (0, b * nL + i, 0)),
            pl.BlockSpec((1, L, LANE), lambda b, i: (SLAB_SMALL, b * nL + i, 0)),
            pl.BlockSpec((3 * H, CONV_WIDTH, LANE), lambda b, i: (0, 0, 0)),
            vec, vec, vec,
        ],
        out_specs=pl.BlockSpec((H, L, LANE), lambda b, i: (0, b * nL + i, 0)),
        scratch_shapes=[
            pltpu.VMEM((3 * H, L + 8, LANE), F32), pltpu.VMEM((H, HEAD_DIM, HEAD_DIM), F32),
            pltpu.VMEM((L, LANE), F32), pltpu.VMEM((L, LANE), F32), pltpu.VMEM((LANE, L), F32),
            pltpu.VMEM((L, LANE), F32),
        ],
        compiler_params=_params(("parallel", "arbitrary")),
        name="deltanet",
    )(P, P, cw, alog, dtb, norm_g.reshape(1, LANE))


def _gelu_tanh(x):
    return 0.5 * x * (1.0 + jnp.tanh(math.sqrt(2.0 / math.pi) * (x + 0.044715 * (x * x * x))))


def _compress_kernel(x_ref, pos_ref, w1_ref, w2_ref, o_ref, ot_ref):
    n = o_ref.shape[3]
    half = CMP_LEN // 2
    y1 = jnp.zeros((n, HEAD_DIM), F32)
    y2 = jnp.zeros((n, HEAD_DIM), F32)
    for l in range(half):
        xl = x_ref[0, pl.ds(l, n, stride=CMP_STRIDE), :]
        y1 = y1 + _dot((xl + pos_ref[0, l:l + 1, :]).astype(BF16), w1_ref[0, l].astype(BF16))
        y2 = y2 + _dot((xl + pos_ref[0, half + l:half + l + 1, :]).astype(BF16),
                       w1_ref[0, half + l].astype(BF16))
    hid = _gelu_tanh(y1 + pltpu.roll(y2, n - 1, 0))
    row = lax.broadcasted_iota(I32, (n, 1), 0)
    hb = jnp.where(row < n - 1, hid, 0.0).astype(BF16)
    w2 = w2_ref[0]
    o_ref[0, 0, 0] = _dot(hb, w2.astype(BF16)).astype(o_ref.dtype)
    ot_ref[0, 0, 0] = _dot_nt(w2.T.astype(BF16), hb).astype(ot_ref.dtype)


def _compress(P, B, S, pos, w1, w2):
    n = S // CMP_STRIDE
    G = NSA_KV_GROUPS
    return pl.pallas_call(
        _compress_kernel,
        out_shape=[jax.ShapeDtypeStruct((2, B, G, n, HEAD_DIM), BF16),
                   jax.ShapeDtypeStruct((2, B, G, HEAD_DIM, n), BF16)],
        grid=(2, B, G),
        in_specs=[
            pl.BlockSpec((1, S, LANE), lambda c, b, g: (SLAB_K_CMP + 2 * c + g, b, 0)),
            pl.BlockSpec((1, CMP_LEN, HEAD_DIM), lambda c, b, g: (c, 0, 0)),
            pl.BlockSpec((1, CMP_LEN, HEAD_DIM, HEAD_DIM), lambda c, b, g: (c, 0, 0, 0)),
            pl.BlockSpec((1, HEAD_DIM, HEAD_DIM), lambda c, b, g: (c, 0, 0)),
        ],
        out_specs=[pl.BlockSpec((1, 1, 1, n, HEAD_DIM), lambda c, b, g: (c, b, g, 0, 0)),
                   pl.BlockSpec((1, 1, 1, HEAD_DIM, n), lambda c, b, g: (c, b, g, 0, 0))],
        compiler_params=_params(("parallel", "parallel", "parallel")),
        name="compress",
    )(P, pos, w1, w2)


LOG2E = math.log2(math.e)
M_INIT = -1e29


def _q_transposed(q_ref):
    c1 = (HEAD_DIM ** -0.5) * LOG2E
    return jnp.concatenate([(q_ref[hh] * c1).T for hh in range(q_ref.shape[0])], axis=1).astype(BF16)


def _nsa_select_kernel(slopes_ref, q_ref, kc_ref, vct_ref, aggt_ref, grp_ref, oct_ref, selt_ref, act_ref):
    g = pl.program_id(1)
    blk = pl.program_id(2)
    Q, H = Q_BLOCK, NSA_HPG
    n = kc_ref.shape[3]
    ns = selt_ref.shape[3]
    st = _dot(kc_ref[0, 0, 0], _q_transposed(q_ref))
    vct = vct_ref[0, 0, 0]

    t = blk * Q + lax.broadcasted_iota(I32, (1, Q), 1)
    cpos = lax.broadcasted_iota(I32, (n, Q), 0) * CMP_STRIDE + (CMP_LEN - 1)
    valid = cpos <= t
    rel = (cpos - (blk * Q + Q - 1)).astype(F32)

    psum = jnp.zeros((n, Q), F32)
    for hh in range(H):
        sh = jnp.where(valid, st[:, hh * Q:(hh + 1) * Q] + rel * (slopes_ref[g * H + hh] * LOG2E), NEG_INF)
        e = jnp.where(valid, jnp.exp2(sh - jnp.max(sh, axis=0, keepdims=True)), 0.0)
        p = e * (1.0 / jnp.maximum(jnp.sum(e, axis=0, keepdims=True), 1e-30))
        oct_ref[0, 0, 0, hh * HEAD_DIM:(hh + 1) * HEAD_DIM, :] = _dot(vct, p.astype(BF16))
        psum = psum + p

    hi = psum.astype(BF16)
    lo = (psum - hi.astype(F32)).astype(BF16)
    imp = _dot(aggt_ref[...], hi) + _dot(aggt_ref[...], lo)

    sid = lax.broadcasted_iota(I32, (ns, Q), 0)
    sidf = sid.astype(F32)
    cur = t // SEL_LEN
    forced = (sid == 0) | (sid == cur) | (sid == cur - 1)
    valid_s = sid * SEL_LEN <= t
    work = jnp.where(forced, SEL_FORCE, jnp.where(valid_s, imp, NEG_INF))
    picked = jnp.zeros((ns, Q), F32)
    for _ in range(min(SEL_COUNT, ns)):
        m_w = jnp.max(work, axis=0, keepdims=True)
        first = jnp.min(jnp.where(work == m_w, sidf, float(ns)), axis=0, keepdims=True)
        hit = sidf == first
        picked = jnp.where(hit, 1.0, picked)
        work = jnp.where(hit, PICKED, work)
    selt = jnp.where(valid_s, picked, 0.0)
    selt_ref[0, 0, 0] = selt
    per_block = _dot_nt(jnp.ones((8, Q), BF16), selt.astype(BF16))
    tiles = _dot(per_block.astype(BF16), grp_ref[...])
    act_ref[0, 0, 0] = (tiles > 0.5).astype(I32)


def _cmp_to_sel_matrix(n_cmp_padded, n_sel):
    r = SEL_LEN // CMP_STRIDE
    c = CMP_LEN // CMP_STRIDE
    off = np.arange(n_cmp_padded)[:, None] - r * np.arange(n_sel)[None, :] + (c - 1)
    cnt = np.minimum(c - 1, off) - np.maximum(0, off - r + 1) + 1
    return np.clip(cnt, 0, None).astype(np.float32)


def _alibi_slopes(n_heads):
    return (2.0 ** (-8.0 * (np.arange(n_heads) + 1) / n_heads)).astype(np.float32)


def _nsa_select(P, kvc, kvct, B, S):
    G, Q, H = NSA_KV_GROUPS, Q_BLOCK, NSA_HPG
    nq = S // Q
    n = S // CMP_STRIDE
    ns = S // SEL_LEN
    aggt = jnp.asarray(_cmp_to_sel_matrix(n, ns).T, BF16)
    per_tile = KEY_TILE // SEL_LEN
    grp = jnp.asarray((np.arange(ns)[:, None] // per_tile) == np.arange(LANE)[None, :], BF16)
    slopes = jnp.asarray(_alibi_slopes(NSA_HEADS))
    per_blk = lambda b, g, i, s_: (b, g, i, 0, 0)
    grid_spec = pltpu.PrefetchScalarGridSpec(
        num_scalar_prefetch=1,
        grid=(B, G, nq),
        in_specs=[
            pl.BlockSpec((H, Q, LANE), lambda b, g, i, s_: (SLAB_NSA_Q // H + g, b * nq + i, 0)),
            pl.BlockSpec((1, 1, 1, n, HEAD_DIM), lambda b, g, i, s_: (0, b, g, 0, 0)),
            pl.BlockSpec((1, 1, 1, HEAD_DIM, n), lambda b, g, i, s_: (1, b, g, 0, 0)),
            pl.BlockSpec((ns, n), lambda b, g, i, s_: (0, 0)),
            pl.BlockSpec((ns, LANE), lambda b, g, i, s_: (0, 0)),
        ],
        out_specs=[
            pl.BlockSpec((1, 1, 1, H * HEAD_DIM, Q), per_blk),
            pl.BlockSpec((1, 1, 1, ns, Q), per_blk),
            pl.BlockSpec((1, 1, 1, 8, LANE), per_blk),
        ],
    )
    return pl.pallas_call(
        _nsa_select_kernel,
        out_shape=[
            jax.ShapeDtypeStruct((B, G, nq, H * HEAD_DIM, Q), F32),
            jax.ShapeDtypeStruct((B, G, nq, ns, Q), F32),
            jax.ShapeDtypeStruct((B, G, nq, 8, LANE), I32),
        ],
        grid_spec=grid_spec,
        compiler_params=_params(("parallel", "parallel", "parallel")),
        name="nsa_select",
    )(slopes, P, kvc, kvct, aggt, grp)


def _nsa_attend_kernel(slopes_ref, act_ref, q_ref, ks_ref, vst_ref, kw_ref, vwt_ref, selt_ref, oct_ref,
                       sm_ref, ng_ref, o_ref, m_scr, l_scr, acc_scr, sig_scr, *, tiles_max):
    b = pl.program_id(0)
    g = pl.program_id(1)
    blk = pl.program_id(2)
    nq = pl.num_programs(2)
    Q, H = Q_BLOCK, NSA_HPG
    per_tile = KEY_TILE // SEL_LEN

    def per_head(x):
        return jnp.concatenate([x] * H, axis=1)

    qt = _q_transposed(q_ref)
    t = blk * Q + lax.broadcasted_iota(I32, (1, Q), 1)
    t_last = blk * Q + Q - 1
    slope_row = jnp.concatenate(
        [jnp.full((1, Q), slopes_ref[g * H + hh] * LOG2E, F32) for hh in range(H)], axis=1)
    key_in_tile = lax.broadcasted_iota(I32, (KEY_TILE, Q), 0)
    diag = blk // (KEY_TILE // Q)

    def reset():
        m_scr[...] = jnp.full_like(m_scr, M_INIT)
        l_scr[...] = jnp.zeros_like(l_scr)
        acc_scr[...] = jnp.zeros_like(acc_scr)

    def update(k_ref, vt_ref, tiles):
        scores = []
        for j, mask_fn in tiles:
            start = pl.multiple_of(j * KEY_TILE, KEY_TILE)
            pos = start + key_in_tile
            s = _dot(k_ref[0, pl.ds(start, KEY_TILE), :], qt) + per_head((pos - t_last).astype(F32)) * slope_row
            scores.append(jnp.where(per_head(mask_fn(pos)), s, NEG_INF))
        m_old = m_scr[...]
        m_new = m_old
        for s in scores:
            m_new = jnp.maximum(m_new, jnp.max(s, axis=0, keepdims=True))
        alpha = jnp.exp2(m_old - m_new)
        l = alpha * l_scr[...]
        acc = alpha * acc_scr[...]
        for (j, _), s in zip(tiles, scores):
            p = jnp.exp2(s - m_new)
            l = l + jnp.sum(p, axis=0, keepdims=True)
            acc = acc + _dot(vt_ref[0, j], p.astype(BF16))
        l_scr[...] = l
        acc_scr[...] = acc
        m_scr[...] = m_new

    def finish():
        return acc_scr[...] * (1.0 / jnp.maximum(l_scr[...], 1e-30))

    def picked(rows):
        spread = jnp.concatenate(
            [jnp.broadcast_to(rows[c:c + 1], (SEL_LEN, Q)) for c in range(per_tile)], axis=0)
        return lambda pos: jnp.where(pos <= t, spread, 0.0) > 0.5

    reset()
    act_base = ((b * pl.num_programs(1) + g) * nq + blk) * tiles_max

    def pair_body(jp, carry):
        @pl.when((act_ref[act_base + 2 * jp] > 0) | (act_ref[act_base + 2 * jp + 1] > 0))
        def _():
            rows = selt_ref[0, 0, 0, pl.ds(pl.multiple_of(jp * 2 * per_tile, 2 * per_tile), 2 * per_tile), :]
            update(ks_ref, vst_ref, [(2 * jp, picked(rows[0:per_tile])),
                                     (2 * jp + 1, picked(rows[per_tile:2 * per_tile]))])
        return carry

    lax.fori_loop(0, diag // 2 + 1, pair_body, 0)
    o_sel = finish()

    def in_window(exists):
        def mask(pos):
            dist = jnp.where(pos <= t, t - pos, WINDOW)
            return jnp.where(exists, dist, WINDOW) < WINDOW
        return mask

    reset()
    update(kw_ref, vwt_ref, [(jnp.maximum(diag - back, 0), in_window(diag - back >= 0))
                             for back in range(WINDOW // KEY_TILE, -1, -1)])
    o_win = finish()

    sig_scr[...] = _sigmoid(sm_ref[0].T)
    for hh in range(H):
        cols = slice(hh * Q, (hh + 1) * Q)
        gl = LANE_GATE + (g * H + hh) * 3
        o = (sig_scr[pl.ds(gl, 1), :] * oct_ref[0, 0, 0, hh * HEAD_DIM:(hh + 1) * HEAD_DIM, :]
             + sig_scr[pl.ds(gl + 1, 1), :] * o_sel[:, cols] + sig_scr[pl.ds(gl + 2, 1), :] * o_win[:, cols])
        on = o * lax.rsqrt(jnp.mean(o * o, axis=0, keepdims=True) + NORM_EPS) * ng_ref[...]
        o_ref[:, hh * HEAD_DIM:(hh + 1) * HEAD_DIM] = on.T.astype(o_ref.dtype)


def _nsa_attend(P, k_bf16, vt_tiles, selt, act_flat, oct, B, S, norm_g):
    G, Q, H = NSA_KV_GROUPS, Q_BLOCK, NSA_HPG
    nq = S // Q
    ns = S // SEL_LEN
    nt = S // KEY_TILE
    T = B * S
    slopes = jnp.asarray(_alibi_slopes(NSA_HEADS))
    ng = jnp.broadcast_to(norm_g[:, None], (HEAD_DIM, Q))
    per_blk = lambda b, g, i, *_: (b, g, i, 0, 0)

    def k_spec(base):
        return pl.BlockSpec((1, S, LANE), lambda b, g, i, *_: (base + g, b, 0))

    def vt_spec(base):
        return pl.BlockSpec((1, nt, HEAD_DIM, KEY_TILE), lambda b, g, i, *_: (base + g, b, 0, 0))

    grid_spec = pltpu.PrefetchScalarGridSpec(
        num_scalar_prefetch=2,
        grid=(B, G, nq),
        in_specs=[
            pl.BlockSpec((H, Q, LANE), lambda b, g, i, *_: (SLAB_NSA_Q // H + g, b * nq + i, 0)),
            k_spec(0), vt_spec(0), k_spec(G), vt_spec(G),
            pl.BlockSpec((1, 1, 1, ns, Q), per_blk),
            pl.BlockSpec((1, 1, 1, H * HEAD_DIM, Q), per_blk),
            pl.BlockSpec((1, Q, LANE), lambda b, g, i, *_: (SLAB_SMALL, b * nq + i, 0)),
            pl.BlockSpec((HEAD_DIM, Q), lambda b, g, i, *_: (0, 0)),
        ],
        out_specs=pl.BlockSpec((Q, H * HEAD_DIM), lambda b, g, i, *_: (b * nq + i, g)),
        scratch_shapes=[
            pltpu.VMEM((1, H * Q), F32), pltpu.VMEM((1, H * Q), F32),
            pltpu.VMEM((HEAD_DIM, H * Q), F32), pltpu.VMEM((LANE, Q), F32),
        ],
    )
    return pl.pallas_call(
        functools.partial(_nsa_attend_kernel, tiles_max=nt),
        out_shape=jax.ShapeDtypeStruct((T, NSA_HEADS * HEAD_DIM), BF16),
        grid_spec=grid_spec,
        compiler_params=_params(("parallel", "parallel", "arbitrary")),
        name="nsa_attend",
    )(slopes, act_flat, P, k_bf16, vt_tiles, k_bf16, vt_tiles, selt, oct, P, ng)


def _kv_prep_kernel(k_ref, v_ref, ko_ref, vo_ref):
    ko_ref[0] = k_ref[0].astype(ko_ref.dtype)
    for c in range(vo_ref.shape[1]):
        vo_ref[0, c] = v_ref[0, c * KEY_TILE:(c + 1) * KEY_TILE, :].T.astype(vo_ref.dtype)


def _attend_operands(P, B, S, tm=1024):
    G = NSA_KV_GROUPS
    T = B * S
    tm = min(tm, S)
    per_step = tm // KEY_TILE

    def slab(first):
        return lambda s, i: (first + (s // G) * (SLAB_K_WIN - SLAB_K_SEL) + s % G, i, 0)

    return pl.pallas_call(
        _kv_prep_kernel,
        out_shape=[jax.ShapeDtypeStruct((2 * G, T, LANE), BF16),
                   jax.ShapeDtypeStruct((2 * G, T // KEY_TILE, HEAD_DIM, KEY_TILE), BF16)],
        grid=(2 * G, T // tm),
        in_specs=[pl.BlockSpec((1, tm, LANE), slab(SLAB_K_SEL)), pl.BlockSpec((1, tm, LANE), slab(SLAB_V_SEL))],
        out_specs=[pl.BlockSpec((1, tm, LANE), lambda s, i: (s, i, 0)),
                   pl.BlockSpec((1, per_step, HEAD_DIM, KEY_TILE), lambda s, i: (s, i, 0, 0))],
        compiler_params=_params(("parallel", "parallel")),
        name="kv_prep",
    )(P, P)


def _active_tiles(act, S):
    return act[:, :, :, 0, :S // KEY_TILE].reshape(-1)


def _out_proj_kernel(ydn_ref, ynsa_ref, x_ref, wo_ref, g_ref, wrh_ref, wrl_ref, br_ref,
                     x1_ref, h2_ref, ri_ref, rw_ref, cnt_ref, carry_scr):
    tm = x_ref.shape[0]
    half = ynsa_ref.shape[1]

    @pl.when(pl.program_id(0) == 0)
    def _():
        carry_scr[...] = jnp.zeros_like(carry_scr)

    ydn = jnp.concatenate([ydn_ref[hh] for hh in range(ydn_ref.shape[0])], axis=1)
    x1 = x_ref[...] + _dot(ydn, wo_ref[0:half, :]) + _dot(ynsa_ref[...], wo_ref[half:, :])
    x1_ref[...] = x1
    h2 = x1 * lax.rsqrt(jnp.mean(x1 * x1, axis=-1, keepdims=True) + NORM_EPS) * g_ref[...]
    h2_ref[...] = h2

    hi = h2.astype(BF16)
    lo = (h2 - hi.astype(F32)).astype(BF16)
    lg = _dot(hi, wrh_ref[...]) + _dot(hi, wrl_ref[...]) + _dot(lo, wrh_ref[...])
    lane = lax.broadcasted_iota(I32, (1, LANE), 1)
    lanef = lane.astype(F32)
    biased = lg + br_ref[...]

    is_grp = lane < N_GROUPS
    gmax = jnp.max(jnp.where(is_grp, lg, NEG_INF), axis=-1, keepdims=True)
    eg = jnp.where(is_grp, jnp.exp(lg - gmax), 0.0)
    pg = eg / jnp.sum(eg, axis=-1, keepdims=True)
    gb = jnp.where(is_grp, biased, NEG_INF)
    g_sel = jnp.min(jnp.where(gb == jnp.max(gb, axis=-1, keepdims=True), lanef, float(LANE)),
                    axis=-1, keepdims=True)
    p_sel = _pick_lane(pg, lanef, g_sel)

    lo_lane = N_GROUPS + g_sel * EXPERTS_PER_GROUP
    in_grp = (lanef >= lo_lane) & (lanef < lo_lane + EXPERTS_PER_GROUP)
    v1 = jnp.where(in_grp, biased, NEG_INF)
    i1 = jnp.min(jnp.where(v1 == jnp.max(v1, axis=-1, keepdims=True), lanef, float(LANE)),
                 axis=-1, keepdims=True)
    v2 = jnp.where(lanef == i1, NEG_INF * 2, v1)
    i2 = jnp.min(jnp.where(v2 == jnp.max(v2, axis=-1, keepdims=True), lanef, float(LANE)),
                 axis=-1, keepdims=True)
    el1 = _pick_lane(lg, lanef, i1)
    el2 = _pick_lane(lg, lanef, i2)
    mx = jnp.maximum(el1, el2)
    e1 = jnp.exp(el1 - mx)
    e2 = jnp.exp(el2 - mx)
    rw_ref[...] = jnp.where(lane == 0, p_sel * (e1 / (e1 + e2)),
                            jnp.where(lane == 1, p_sel * (e2 / (e1 + e2)), 0.0))

    ex1 = i1 - N_GROUPS
    ex2 = i2 - N_GROUPS
    oh1 = lanef == ex1
    oh2 = lanef == ex2
    both = jnp.where(oh1 | oh2, 1.0, 0.0)
    ri = lax.broadcasted_iota(I32, (tm, tm), 0)
    ci = lax.broadcasted_iota(I32, (tm, tm), 1)
    before = _dot((ci < ri).astype(BF16), both.astype(BF16)) + carry_scr[...]
    r1 = jnp.sum(jnp.where(oh1, before, 0.0), axis=-1, keepdims=True)
    r2 = jnp.sum(jnp.where(oh2, before, 0.0), axis=-1, keepdims=True)
    packed = jnp.where(lane == 0, ex1, jnp.where(lane == 1, ex2,
                       jnp.where(lane == 2, r1, jnp.where(lane == 3, r2, 0.0))))
    ri_ref[...] = packed.astype(I32)
    total = carry_scr[...] + jnp.sum(both, axis=0, keepdims=True)
    carry_scr[...] = total
    cnt_ref[...] = jnp.broadcast_to(total, cnt_ref.shape).astype(I32)


def _out_proj(ydn, ynsa, x2d, wo, g, wr_hi, wr_lo, br, tm=256):
    T, D = x2d.shape
    half = ynsa.shape[1]
    row = lambda i: (i, 0)
    fixed = lambda i: (0, 0)
    return pl.pallas_call(
        _out_proj_kernel,
        out_shape=[
            jax.ShapeDtypeStruct((T, D), F32), jax.ShapeDtypeStruct((T, D), F32),
            jax.ShapeDtypeStruct((T, LANE), I32), jax.ShapeDtypeStruct((T, LANE), F32),
            jax.ShapeDtypeStruct((8, LANE), I32),
        ],
        grid=(T // tm,),
        in_specs=[
            pl.BlockSpec((ydn.shape[0], tm, LANE), lambda i: (0, i, 0)),
            pl.BlockSpec((tm, half), row), pl.BlockSpec((tm, D), row),
            pl.BlockSpec((D, D), fixed), pl.BlockSpec((1, D), fixed),
            pl.BlockSpec((D, LANE), fixed), pl.BlockSpec((D, LANE), fixed), pl.BlockSpec((1, LANE), fixed),
        ],
        out_specs=[
            pl.BlockSpec((tm, D), row), pl.BlockSpec((tm, D), row),
            pl.BlockSpec((tm, LANE), row), pl.BlockSpec((tm, LANE), row), pl.BlockSpec((8, LANE), fixed),
        ],
        scratch_shapes=[pltpu.VMEM((1, LANE), F32)],
        compiler_params=_params(("arbitrary",)),
        name="out_proj_router",
    )(ydn, ynsa, x2d, wo, g, wr_hi, wr_lo, br)


def _row_copy(src_hbm, row, dst, slot, sem):
    return pltpu.make_async_copy(src_hbm.at[pl.ds(row, 1)], dst.at[pl.ds(slot, 1)], sem)


GATHER_UNROLL = 8


def _gather_rows(src_hbm, idx_ref, base, dst, sem, n_rows):
    def issue(r, c):
        _row_copy(src_hbm, idx_ref[base + r], dst, r, sem).start()
        return c
    lax.fori_loop(0, n_rows, issue, 0, unroll=GATHER_UNROLL)


def _wait_rows(src_hbm, dst, sem):
    pltpu.make_async_copy(src_hbm.at[pl.ds(0, dst.shape[0])], dst, sem).wait()


def _moe_dispatch_kernel(dest_ref, pad_ref, nu_ref, h_hbm, xs_hbm, zero_scr, sem, zsem, *, tm, n_blocks):
    i = pl.program_id(0)
    n_steps = pl.num_programs(0)
    T = tm * n_steps
    R = zero_scr.shape[0]
    slot = i % 2

    def zero_fill(act):
        def expert_pad(e, c):
            def pad_row(r, c2):
                act(_row_copy(zero_scr, 0, xs_hbm, pad_ref[2 * e] + r, zsem))
                return c2
            return lax.fori_loop(0, pad_ref[2 * e + 1], pad_row, c)
        lax.fori_loop(0, N_EXPERTS, expert_pad, 0)

        def unused_block(bk, c):
            act(pltpu.make_async_copy(zero_scr, xs_hbm.at[pl.ds(pl.multiple_of(bk * R, R), R)], zsem))
            return c
        lax.fori_loop(nu_ref[0], n_blocks, unused_block, 0)

    def wait_step(s):
        pltpu.make_async_copy(h_hbm.at[pl.ds(0, 2 * tm)], xs_hbm.at[pl.ds(0, 2 * tm)], sem.at[s]).wait()

    @pl.when(i == 0)
    def _():
        zero_scr[...] = jnp.zeros_like(zero_scr)
        zero_fill(lambda cp: cp.start())

    def issue(r, c):
        t = i * tm + r
        for k in range(2):
            pltpu.make_async_copy(h_hbm.at[pl.ds(t, 1)], xs_hbm.at[pl.ds(dest_ref[k * T + t], 1)],
                                  sem.at[slot]).start()
        return c
    lax.fori_loop(0, tm, issue, 0, unroll=GATHER_UNROLL)

    @pl.when(i > 0)
    def _():
        wait_step(1 - slot)

    @pl.when(i == n_steps - 1)
    def _():
        wait_step(slot)
        zero_fill(lambda cp: cp.wait())


def _moe_dispatch(h2, dest_flat, pad_table, n_used, n_blocks, tm=256):
    T, D = h2.shape
    R = MOE_ROWS
    grid_spec = pltpu.PrefetchScalarGridSpec(
        num_scalar_prefetch=3,
        grid=(T // tm,),
        in_specs=[pl.BlockSpec(memory_space=pl.ANY)],
        out_specs=pl.BlockSpec(memory_space=pl.ANY),
        scratch_shapes=[pltpu.VMEM((R, D), F32), pltpu.SemaphoreType.DMA((2,)), pltpu.SemaphoreType.DMA],
    )
    return pl.pallas_call(
        functools.partial(_moe_dispatch_kernel, tm=tm, n_blocks=n_blocks),
        out_shape=jax.ShapeDtypeStruct((n_blocks * R, D), F32),
        grid_spec=grid_spec,
        compiler_params=_params(("arbitrary",)),
        name="moe_dispatch",
    )(dest_flat, pad_table, n_used, h2)


def _moe_experts_kernel(be_ref, nu_ref, xs_ref, wg_ref, wu_ref, wd_ref, y_ref, wg_scr, wu_scr, wd_scr):
    i = pl.program_id(0)

    @pl.when(i < nu_ref[0])
    def _():
        @pl.when((i == 0) | (be_ref[jnp.maximum(i - 1, 0)] != be_ref[i]))
        def _():
            wg_scr[...] = wg_ref[0].astype(BF16)
            wu_scr[...] = wu_ref[0].astype(BF16)
            wd_scr[...] = wd_ref[0].astype(BF16)

        xb = xs_ref[...].astype(BF16)
        hid = _silu(_dot(xb, wg_scr[...])) * _dot(xb, wu_scr[...])
        y_ref[...] = _dot(hid.astype(BF16), wd_scr[...])

    @pl.when(i >= nu_ref[0])
    def _():
        y_ref[...] = jnp.zeros_like(y_ref)


def _moe_experts(xs, block_expert, n_used, w_gate, w_up, w_down):
    D = xs.shape[1]
    R = MOE_ROWS
    n_blocks = block_expert.shape[0]
    DE = w_gate.shape[2]
    grid_spec = pltpu.PrefetchScalarGridSpec(
        num_scalar_prefetch=2,
        grid=(n_blocks,),
        in_specs=[
            pl.BlockSpec((R, D), lambda i, be, nu: (jnp.maximum(jnp.minimum(i, nu[0] - 1), 0), 0)),
            pl.BlockSpec((1, D, DE), lambda i, be, nu: (be[i], 0, 0)),
            pl.BlockSpec((1, D, DE), lambda i, be, nu: (be[i], 0, 0)),
            pl.BlockSpec((1, DE, D), lambda i, be, nu: (be[i], 0, 0)),
        ],
        out_specs=pl.BlockSpec((R, D), lambda i, be, nu: (i, 0)),
        scratch_shapes=[pltpu.VMEM((D, DE), BF16), pltpu.VMEM((D, DE), BF16), pltpu.VMEM((DE, D), BF16)],
    )
    return pl.pallas_call(
        _moe_experts_kernel,
        out_shape=jax.ShapeDtypeStruct((n_blocks * R, D), F32),
        grid_spec=grid_spec,
        compiler_params=_params(("arbitrary",)),
        name="moe_experts",
    )(block_expert, n_used, xs, w_gate, w_up, w_down)


def _moe_combine_kernel(dest_ref, x1_ref, rw_ref, g_ref, ys_hbm, o_ref, buf, sem):
    i = pl.program_id(0)
    tm = x1_ref.shape[0]
    T = tm * pl.num_programs(0)
    slot = i % 2

    def start(step, s):
        for k in range(2):
            _gather_rows(ys_hbm, dest_ref, k * T + step * tm, buf.at[s, k], sem.at[s], tm)

    @pl.when(i == 0)
    def _():
        start(0, 0)

    @pl.when(i + 1 < pl.num_programs(0))
    def _():
        start(i + 1, 1 - slot)

    for k in range(2):
        _wait_rows(ys_hbm, buf.at[slot, k], sem.at[slot])
    rw = rw_ref[...]
    x2 = x1_ref[...] + buf[slot, 0] * rw[:, 0:1] + buf[slot, 1] * rw[:, 1:2]
    o_ref[...] = x2 * lax.rsqrt(jnp.mean(x2 * x2, axis=-1, keepdims=True) + NORM_EPS) * g_ref[...]


def _moe_combine(dest_flat, x1, rw, g, ys, tm=128):
    T, D = x1.shape
    grid_spec = pltpu.PrefetchScalarGridSpec(
        num_scalar_prefetch=1,
        grid=(T // tm,),
        in_specs=[
            pl.BlockSpec((tm, D), lambda i, d: (i, 0)),
            pl.BlockSpec((tm, LANE), lambda i, d: (i, 0)),
            pl.BlockSpec((1, D), lambda i, d: (0, 0)),
            pl.BlockSpec(memory_space=pl.ANY),
        ],
        out_specs=pl.BlockSpec((tm, D), lambda i, d: (i, 0)),
        scratch_shapes=[pltpu.VMEM((2, 2, tm, D), F32), pltpu.SemaphoreType.DMA((2,))],
    )
    return pl.pallas_call(
        _moe_combine_kernel,
        out_shape=jax.ShapeDtypeStruct((T, D), F32),
        grid_spec=grid_spec,
        compiler_params=_params(("arbitrary",)),
        name="moe_combine",
    )(dest_flat, x1, rw, g, ys)


def _slab_weights(w_in):
    D = w_in.shape[0]
    n_dn = 4 * DN_HEADS * HEAD_DIM
    n_small_a = 2 * DN_HEADS
    n_nsa = NSA_HEADS * HEAD_DIM + 6 * NSA_KV_GROUPS * HEAD_DIM
    c0 = n_dn + n_small_a
    c1 = c0 + n_nsa
    n_gate = 3 * NSA_HEADS
    pad = N_SLABS * LANE - (n_dn + n_nsa + n_small_a + n_gate)
    return jnp.concatenate(
        [w_in[:, :n_dn], w_in[:, c0:c1], w_in[:, n_dn:c0], w_in[:, c1:c1 + n_gate],
         jnp.zeros((D, pad), w_in.dtype)], axis=1).astype(BF16)


def _layer(x, norm_mix_g, w_in, conv_w, dt_bias, a_log, dn_norm_g, cmp_pos_k, cmp_w1_k, cmp_w2_k,
           cmp_pos_v, cmp_w1_v, cmp_w2_v, nsa_norm_g, w_out, norm_ffn_g, w_group, b_group,
           w_router, b_router, w_gate, w_up, w_down, norm_final_g):
    B, S, D = x.shape
    T = B * S
    x2d = x.reshape(T, D)

    P = _in_proj(x2d, norm_mix_g.reshape(1, D), _slab_weights(w_in))
    ydn = _deltanet(P, B, S, conv_w, dt_bias, a_log, dn_norm_g)

    kvc, kvct = _compress(P, B, S, jnp.stack([cmp_pos_k, cmp_pos_v]), jnp.stack([cmp_w1_k, cmp_w1_v]),
                          jnp.stack([cmp_w2_k, cmp_w2_v]))
    oct, selt, act = _nsa_select(P, kvc, kvct, B, S)
    ynsa = _nsa_attend(P, *_attend_operands(P, B, S), selt, _active_tiles(act, S), oct, B, S, nsa_norm_g)

    wr = jnp.zeros((D, LANE), F32).at[:, :N_GROUPS].set(w_group).at[:, N_GROUPS:N_GROUPS + N_EXPERTS].set(w_router)
    wr_hi = wr.astype(BF16)
    wr_lo = (wr - wr_hi.astype(F32)).astype(BF16)
    br = jnp.zeros((1, LANE), F32).at[0, :N_GROUPS].set(b_group).at[0, N_GROUPS:N_GROUPS + N_EXPERTS].set(b_router)
    x1, h2, ri, rw, cnt = _out_proj(ydn, ynsa, x2d, w_out.astype(BF16), norm_ffn_g.reshape(1, D),
                                    wr_hi, wr_lo, br)

    R = MOE_ROWS
    counts = cnt[0, :N_EXPERTS]
    padded = (counts + R - 1) // R * R
    pends = jnp.cumsum(padded)
    pstarts = pends - padded
    first_row = jnp.sum(jnp.where(ri[:, 0:2, None] == jnp.arange(N_EXPERTS, dtype=I32), pstarts, 0), axis=-1)
    dest = first_row + ri[:, 2:4]
    dest_flat = dest.T.reshape(-1).astype(I32)
    n_blocks = (2 * T + R - 1) // R + N_EXPERTS
    block_start = jnp.arange(n_blocks, dtype=I32) * R
    block_expert = jnp.minimum(jnp.sum(block_start[:, None] >= pends[None, :], axis=1),
                               N_EXPERTS - 1).astype(I32)
    n_used = (pends[-1:] // R).astype(I32)
    pad_table = jnp.stack([pstarts + counts, padded - counts], axis=1).reshape(-1).astype(I32)

    xs = _moe_dispatch(h2, dest_flat, pad_table, n_used, n_blocks)
    ys = _moe_experts(xs, block_expert, n_used, w_gate, w_up, w_down)
    out = _moe_combine(dest_flat, x1, rw, norm_final_g.reshape(1, D), ys)
    return out.reshape(B, S, D)


def kernel(x, norm_mix_g, w_in, conv_w, dt_bias, a_log, dn_norm_g, cmp_pos_k, cmp_w1_k, cmp_w2_k,
           cmp_pos_v, cmp_w1_v, cmp_w2_v, nsa_norm_g, w_out, norm_ffn_g, w_group, b_group,
           w_router, b_router, w_gate, w_up, w_down, norm_final_g):
    assert w_in.shape[0] == 1, "one layer"
    return _layer(x, norm_mix_g[0], w_in[0], conv_w[0], dt_bias[0], a_log[0], dn_norm_g[0],
                  cmp_pos_k[0], cmp_w1_k[0], cmp_w2_k[0], cmp_pos_v[0], cmp_w1_v[0], cmp_w2_v[0],
                  nsa_norm_g[0], w_out[0], norm_ffn_g[0], w_group[0], b_group[0], w_router[0],
                  b_router[0], w_gate[0], w_up[0], w_down[0], norm_final_g)
```

```python
import functools
import math

import numpy as np
import jax
import jax.numpy as jnp
from jax import lax
from jax.experimental import pallas as pl
from jax.experimental.pallas import tpu as pltpu

F32 = jnp.float32
BF16 = jnp.bfloat16
I32 = jnp.int32

LANE = 128
HEAD_DIM = 128
DN_HEADS = 8
NSA_HEADS = 8
NSA_KV_GROUPS = 2
NSA_HPG = NSA_HEADS // NSA_KV_GROUPS
CONV_WIDTH = 4
DN_CHUNK = 64
CMP_LEN = 32
CMP_STRIDE = 16
SEL_LEN = 64
SEL_COUNT = 16
WINDOW = 512
Q_BLOCK = 128
N_GROUPS = 8
EXPERTS_PER_GROUP = 8
N_EXPERTS = N_GROUPS * EXPERTS_PER_GROUP
NORM_EPS = 1e-6
NEG_INF = -1e30
SEL_FORCE = 1e9
PICKED = -3e38

SLAB_DN_Q, SLAB_DN_K, SLAB_DN_V, SLAB_DN_Z = 0, 8, 16, 24
SLAB_NSA_Q = 32
SLAB_K_CMP, SLAB_V_CMP, SLAB_K_SEL, SLAB_V_SEL, SLAB_K_WIN, SLAB_V_WIN = 40, 42, 44, 46, 48, 50
SLAB_SMALL = 52
N_SLABS = 54
LANE_DN_B, LANE_DN_A, LANE_GATE = 0, 8, 16

KEY_TILE = 256
MOE_ROWS = 256

VMEM_LIMIT = 56 * 1024 * 1024


def _params(sem, vmem=VMEM_LIMIT):
    return pltpu.CompilerParams(dimension_semantics=sem, vmem_limit_bytes=vmem)


def _dot(a, b):
    return jnp.dot(a, b, preferred_element_type=F32)


def _dot_nt(a, b):
    return lax.dot_general(a, b, (((1,), (1,)), ((), ())), preferred_element_type=F32)


def _dot_exact(a, b):
    return jnp.dot(a, b, preferred_element_type=F32, precision=lax.Precision.HIGHEST)


def _sigmoid(x):
    return 1.0 / (1.0 + jnp.exp(-x))


def _silu(x):
    return x * _sigmoid(x)


def _pick_lane(x, lane_ids, idx):
    return jnp.sum(jnp.where(lane_ids == idx, x, 0.0), axis=1, keepdims=True)


def _in_proj_kernel(x_ref, g_ref, w_ref, o_ref, h_scr):
    @pl.when(pl.program_id(1) == 0)
    def _():
        xf = x_ref[...]
        ms = jnp.mean(xf * xf, axis=-1, keepdims=True)
        h_scr[...] = (xf * lax.rsqrt(ms + NORM_EPS) * g_ref[...]).astype(BF16)

    acc = _dot(h_scr[...], w_ref[...])
    for k in range(o_ref.shape[0]):
        o_ref[k] = acc[:, k * LANE:(k + 1) * LANE]


def _in_proj(x2d, g, w_slabs, tm=1024, tn=768):
    T, D = x2d.shape
    NP = w_slabs.shape[1]
    ns = tn // LANE
    return pl.pallas_call(
        _in_proj_kernel,
        out_shape=jax.ShapeDtypeStruct((NP // LANE, T, LANE), F32),
        grid=(T // tm, NP // tn),
        in_specs=[
            pl.BlockSpec((tm, D), lambda i, j: (i, 0)),
            pl.BlockSpec((1, D), lambda i, j: (0, 0)),
            pl.BlockSpec((D, tn), lambda i, j: (0, j)),
        ],
        out_specs=pl.BlockSpec((ns, tm, LANE), lambda i, j: (j, i, 0)),
        scratch_shapes=[pltpu.VMEM((tm, D), BF16)],
        compiler_params=_params(("parallel", "arbitrary")),
        name="in_proj",
    )(x2d, g, w_slabs)


DN_HEADS_PER_ITER = 8


def _deltanet_kernel(p_ref, sm_ref, cw_ref, alog_ref, dtb_ref, ng_ref, o_ref,
                     ext_scr, s_scr, gc_scr, gl_scr, gct_scr, beta_scr):
    L = p_ref.shape[1]
    C = DN_CHUNK
    H = DN_HEADS

    @pl.when(pl.program_id(1) == 0)
    def _():
        ext_scr[:, 0:8, :] = jnp.zeros((3 * H, 8, LANE), F32)
        s_scr[...] = jnp.zeros_like(s_scr)

    sm = sm_ref[0]
    lane = lax.broadcasted_iota(I32, (1, LANE), 1)
    xa = sm + dtb_ref[...]
    softplus = jnp.maximum(xa, 0.0) + jnp.log1p(jnp.exp(-jnp.abs(xa)))
    ld = -jnp.exp(alog_ref[...]) * softplus
    ri = lax.broadcasted_iota(I32, (L, L), 0)
    ci = lax.broadcasted_iota(I32, (L, L), 1)
    same = (ri // C) == (ci // C)
    causal = same & (ci <= ri)
    gc_slab = _dot_exact(jnp.where(causal, 1.0, 0.0), ld)
    gc_scr[...] = gc_slab
    gct_scr[...] = gc_slab.T
    gl_scr[...] = _dot_exact(jnp.where(same, 1.0, 0.0), ld)
    beta_scr[...] = _sigmoid(sm)

    def conv_silu(slab, h):
        idx = slab + h
        ext_scr[idx, 8:, :] = p_ref[idx]
        w = cw_ref[idx]
        acc = ext_scr[idx, pl.ds(8 - (CONV_WIDTH - 1), L), :] * w[0:1]
        for j in range(1, CONV_WIDTH):
            acc = acc + ext_scr[idx, pl.ds(8 - (CONV_WIDTH - 1) + j, L), :] * w[j:j + 1]
        ext_scr[idx, 0:8, :] = ext_scr[idx, L:L + 8, :]
        return _silu(acc)

    def l2n(x):
        return x * lax.rsqrt(jnp.sum(x * x, axis=-1, keepdims=True) + NORM_EPS)

    def head_group(hg, carry):
        hs = [hg * DN_HEADS_PER_ITER + dh for dh in range(DN_HEADS_PER_ITER)]
        n = range(len(hs))
        qs = [l2n(conv_silu(SLAB_DN_Q, h)) * (HEAD_DIM ** -0.5) for h in hs]
        ks = [l2n(conv_silu(SLAB_DN_K, h)) for h in hs]
        vs = [conv_silu(SLAB_DN_V, h) for h in hs]
        betas = [_pick_lane(beta_scr[...], lane, LANE_DN_B + h) for h in hs]
        gcols = [_pick_lane(gc_scr[...], lane, LANE_DN_A + h) for h in hs]
        glasts = [_pick_lane(gl_scr[...], lane, LANE_DN_A + h) for h in hs]
        grows = [gct_scr[pl.ds(LANE_DN_A + h, 1), :] for h in hs]
        states = [s_scr[h] for h in hs]
        zs = [p_ref[SLAB_DN_Z + h] for h in hs]

        decays = [jnp.where(causal, jnp.exp(jnp.where(causal, gcols[i] - grows[i], 0.0)), 0.0) for i in n]
        egcs = [jnp.exp(gcols[i]) for i in n]
        kbs = [ks[i] * betas[i] for i in n]
        kts = [ks[i].astype(BF16) for i in n]
        lms = [jnp.where(ci < ri, _dot_nt(kbs[i].astype(BF16), kts[i]) * decays[i], 0.0) for i in n]
        tinvs = [jnp.where(ri == ci, 1.0, 0.0) - lms[i] for i in n]
        pws = lms
        for _ in range(int(math.log2(C)) - 1):
            pwbs = [pws[i].astype(BF16) for i in n]
            pws = [_dot(pwbs[i], pwbs[i]) for i in n]
            tinvs = [tinvs[i] + _dot(tinvs[i].astype(BF16), pws[i].astype(BF16)) for i in n]
        tbs = [tinvs[i].astype(BF16) for i in n]
        us = [_dot(tbs[i], (vs[i] * betas[i]).astype(BF16)) for i in n]
        ws = [_dot(tbs[i], (kbs[i] * egcs[i]).astype(BF16)) for i in n]
        qks = [_dot_nt(qs[i].astype(BF16), kts[i]) * decays[i] for i in n]
        q_decs = [(qs[i] * egcs[i]).astype(BF16) for i in n]
        k_dec_ts = [(ks[i] * jnp.exp(glasts[i] - gcols[i])).T for i in n]

        outs = [[] for _ in n]
        for c in range(L // C):
            sl = slice(c * C, (c + 1) * C)
            sbs = [states[i].astype(BF16) for i in n]
            vnbs = [(us[i][sl] - _dot(ws[i][sl].astype(BF16), sbs[i])).astype(BF16) for i in n]
            for i in n:
                outs[i].append(_dot(q_decs[i][sl], sbs[i]) + _dot(qks[i][sl, sl].astype(BF16), vnbs[i]))
            states = [states[i] * jnp.exp(glasts[i][c * C:c * C + 1, :])
                      + _dot(k_dec_ts[i][:, sl].astype(BF16), vnbs[i]) for i in n]

        for i, h in enumerate(hs):
            s_scr[h] = states[i]
            o = jnp.concatenate(outs[i], axis=0)
            on = o * lax.rsqrt(jnp.mean(o * o, axis=-1, keepdims=True) + NORM_EPS) * ng_ref[...]
            o_ref[h] = (on * _silu(zs[i])).astype(o_ref.dtype)
        return carry

    lax.fori_loop(0, H // DN_HEADS_PER_ITER, head_group, 0)


def _deltanet(P, B, S, conv_w, dt_bias, a_log, norm_g, L=256):
    T = B * S
    nL = S // L
    H = DN_HEADS
    cw = conv_w.reshape(CONV_WIDTH, 3 * H, LANE).transpose(1, 0, 2)
    alog = jnp.zeros((1, LANE), F32).at[0, LANE_DN_A:LANE_DN_A + H].set(a_log)
    dtb = jnp.zeros((1, LANE), F32).at[0, LANE_DN_A:LANE_DN_A + H].set(dt_bias)
    vec = pl.BlockSpec((1, LANE), lambda b, i: (0, 0))
    return pl.pallas_call(
        _deltanet_kernel,
        out_shape=jax.ShapeDtypeStruct((H, T, LANE), BF16),
        grid=(B, nL),
        in_specs=[
            pl.BlockSpec((4 * H, L, LANE), lambda b, i: (0, b * nL + i, 0)),
            pl.BlockSpec((1, L, LANE), lambda b, i: (SLAB_SMALL, b * nL + i, 0)),
            pl.BlockSpec((3 * H, CONV_WIDTH, LANE), lambda b, i: (0, 0, 0)),
            vec, vec, vec,
        ],
        out_specs=pl.BlockSpec((H, L, LANE), lambda b, i: (0, b * nL + i, 0)),
        scratch_shapes=[
            pltpu.VMEM((3 * H, L + 8, LANE), F32), pltpu.VMEM((H, HEAD_DIM, HEAD_DIM), F32),
            pltpu.VMEM((L, LANE), F32), pltpu.VMEM((L, LANE), F32), pltpu.VMEM((LANE, L), F32),
            pltpu.VMEM((L, LANE), F32),
        ],
        compiler_params=_params(("parallel", "arbitrary")),
        name="deltanet",
    )(P, P, cw, alog, dtb, norm_g.reshape(1, LANE))


def _gelu_tanh(x):
    return 0.5 * x * (1.0 + jnp.tanh(math.sqrt(2.0 / math.pi) * (x + 0.044715 * (x * x * x))))


def _compress_kernel(x_ref, pos_ref, w1_ref, w2_ref, o_ref, ot_ref):
    n = o_ref.shape[3]
    half = CMP_LEN // 2
    y1 = jnp.zeros((n, HEAD_DIM), F32)
    y2 = jnp.zeros((n, HEAD_DIM), F32)
    for l in range(half):
        xl = x_ref[0, pl.ds(l, n, stride=CMP_STRIDE), :]
        y1 = y1 + _dot((xl + pos_ref[0, l:l + 1, :]).astype(BF16), w1_ref[0, l].astype(BF16))
        y2 = y2 + _dot((xl + pos_ref[0, half + l:half + l + 1, :]).astype(BF16),
                       w1_ref[0, half + l].astype(BF16))
    hid = _gelu_tanh(y1 + pltpu.roll(y2, n - 1, 0))
    row = lax.broadcasted_iota(I32, (n, 1), 0)
    hb = jnp.where(row < n - 1, hid, 0.0).astype(BF16)
    w2 = w2_ref[0]
    o_ref[0, 0, 0] = _dot(hb, w2.astype(BF16)).astype(o_ref.dtype)
    ot_ref[0, 0, 0] = _dot_nt(w2.T.astype(BF16), hb).astype(ot_ref.dtype)


def _compress(P, B, S, pos, w1, w2):
    n = S // CMP_STRIDE
    G = NSA_KV_GROUPS
    return pl.pallas_call(
        _compress_kernel,
        out_shape=[jax.ShapeDtypeStruct((2, B, G, n, HEAD_DIM), BF16),
                   jax.ShapeDtypeStruct((2, B, G, HEAD_DIM, n), BF16)],
        grid=(2, B, G),
        in_specs=[
            pl.BlockSpec((1, S, LANE), lambda c, b, g: (SLAB_K_CMP + 2 * c + g, b, 0)),
            pl.BlockSpec((1, CMP_LEN, HEAD_DIM), lambda c, b, g: (c, 0, 0)),
            pl.BlockSpec((1, CMP_LEN, HEAD_DIM, HEAD_DIM), lambda c, b, g: (c, 0, 0, 0)),
            pl.BlockSpec((1, HEAD_DIM, HEAD_DIM), lambda c, b, g: (c, 0, 0)),
        ],
        out_specs=[pl.BlockSpec((1, 1, 1, n, HEAD_DIM), lambda c, b, g: (c, b, g, 0, 0)),
                   pl.BlockSpec((1, 1, 1, HEAD_DIM, n), lambda c, b, g: (c, b, g, 0, 0))],
        compiler_params=_params(("parallel", "parallel", "parallel")),
        name="compress",
    )(P, pos, w1, w2)


LOG2E = math.log2(math.e)
M_INIT = -1e29


def _q_transposed(q_ref):
    c1 = (HEAD_DIM ** -0.5) * LOG2E
    return jnp.concatenate([(q_ref[hh] * c1).T for hh in range(q_ref.shape[0])], axis=1).astype(BF16)


def _nsa_select_kernel(slopes_ref, q_ref, kc_ref, vct_ref, aggt_ref, grp_ref, oct_ref, selt_ref, act_ref):
    g = pl.program_id(1)
    blk = pl.program_id(2)
    Q, H = Q_BLOCK, NSA_HPG
    n = kc_ref.shape[3]
    ns = selt_ref.shape[3]
    st = _dot(kc_ref[0, 0, 0], _q_transposed(q_ref))
    vct = vct_ref[0, 0, 0]

    t = blk * Q + lax.broadcasted_iota(I32, (1, Q), 1)
    cpos = lax.broadcasted_iota(I32, (n, Q), 0) * CMP_STRIDE + (CMP_LEN - 1)
    valid = cpos <= t
    rel = (cpos - (blk * Q + Q - 1)).astype(F32)

    psum = jnp.zeros((n, Q), F32)
    for hh in range(H):
        sh = jnp.where(valid, st[:, hh * Q:(hh + 1) * Q] + rel * (slopes_ref[g * H + hh] * LOG2E), NEG_INF)
        e = jnp.where(valid, jnp.exp2(sh - jnp.max(sh, axis=0, keepdims=True)), 0.0)
        p = e * (1.0 / jnp.maximum(jnp.sum(e, axis=0, keepdims=True), 1e-30))
        oct_ref[0, 0, 0, hh * HEAD_DIM:(hh + 1) * HEAD_DIM, :] = _dot(vct, p.astype(BF16))
        psum = psum + p

    hi = psum.astype(BF16)
    lo = (psum - hi.astype(F32)).astype(BF16)
    imp = _dot(aggt_ref[...], hi) + _dot(aggt_ref[...], lo)

    sid = lax.broadcasted_iota(I32, (ns, Q), 0)
    sidf = sid.astype(F32)
    cur = t // SEL_LEN
    forced = (sid == 0) | (sid == cur) | (sid == cur - 1)
    valid_s = sid * SEL_LEN <= t
    work = jnp.where(forced, SEL_FORCE, jnp.where(valid_s, imp, NEG_INF))
    picked = jnp.zeros((ns, Q), F32)
    for _ in range(min(SEL_COUNT, ns)):
        m_w = jnp.max(work, axis=0, keepdims=True)
        first = jnp.min(jnp.where(work == m_w, sidf, float(ns)), axis=0, keepdims=True)
        hit = sidf == first
        picked = jnp.where(hit, 1.0, picked)
        work = jnp.where(hit, PICKED, work)
    selt = jnp.where(valid_s, picked, 0.0)
    selt_ref[0, 0, 0] = selt
    per_block = _dot_nt(jnp.ones((8, Q), BF16), selt.astype(BF16))
    tiles = _dot(per_block.astype(BF16), grp_ref[...])
    act_ref[0, 0, 0] = (tiles > 0.5).astype(I32)


def _cmp_to_sel_matrix(n_cmp_padded, n_sel):
    r = SEL_LEN // CMP_STRIDE
    c = CMP_LEN // CMP_STRIDE
    off = np.arange(n_cmp_padded)[:, None] - r * np.arange(n_sel)[None, :] + (c - 1)
    cnt = np.minimum(c - 1, off) - np.maximum(0, off - r + 1) + 1
    return np.clip(cnt, 0, None).astype(np.float32)


def _alibi_slopes(n_heads):
    return (2.0 ** (-8.0 * (np.arange(n_heads) + 1) / n_heads)).astype(np.float32)


def _nsa_select(P, kvc, kvct, B, S):
    G, Q, H = NSA_KV_GROUPS, Q_BLOCK, NSA_HPG
    nq = S // Q
    n = S // CMP_STRIDE
    ns = S // SEL_LEN
    aggt = jnp.asarray(_cmp_to_sel_matrix(n, ns).T, BF16)
    per_tile = KEY_TILE // SEL_LEN
    grp = jnp.asarray((np.arange(ns)[:, None] // per_tile) == np.arange(LANE)[None, :], BF16)
    slopes = jnp.asarray(_alibi_slopes(NSA_HEADS))
    per_blk = lambda b, g, i, s_: (b, g, i, 0, 0)
    grid_spec = pltpu.PrefetchScalarGridSpec(
        num_scalar_prefetch=1,
        grid=(B, G, nq),
        in_specs=[
            pl.BlockSpec((H, Q, LANE), lambda b, g, i, s_: (SLAB_NSA_Q // H + g, b * nq + i, 0)),
            pl.BlockSpec((1, 1, 1, n, HEAD_DIM), lambda b, g, i, s_: (0, b, g, 0, 0)),
            pl.BlockSpec((1, 1, 1, HEAD_DIM, n), lambda b, g, i, s_: (1, b, g, 0, 0)),
            pl.BlockSpec((ns, n), lambda b, g, i, s_: (0, 0)),
            pl.BlockSpec((ns, LANE), lambda b, g, i, s_: (0, 0)),
        ],
        out_specs=[
            pl.BlockSpec((1, 1, 1, H * HEAD_DIM, Q), per_blk),
            pl.BlockSpec((1, 1, 1, ns, Q), per_blk),
            pl.BlockSpec((1, 1, 1, 8, LANE), per_blk),
        ],
    )
    return pl.pallas_call(
        _nsa_select_kernel,
        out_shape=[
            jax.ShapeDtypeStruct((B, G, nq, H * HEAD_DIM, Q), F32),
            jax.ShapeDtypeStruct((B, G, nq, ns, Q), F32),
            jax.ShapeDtypeStruct((B, G, nq, 8, LANE), I32),
        ],
        grid_spec=grid_spec,
        compiler_params=_params(("parallel", "parallel", "parallel")),
        name="nsa_select",
    )(slopes, P, kvc, kvct, aggt, grp)


def _nsa_attend_kernel(slopes_ref, act_ref, q_ref, ks_ref, vst_ref, kw_ref, vwt_ref, selt_ref, oct_ref,
                       sm_ref, ng_ref, o_ref, m_scr, l_scr, acc_scr, sig_scr, *, tiles_max):
    b = pl.program_id(0)
    g = pl.program_id(1)
    blk = pl.program_id(2)
    nq = pl.num_programs(2)
    Q, H = Q_BLOCK, NSA_HPG
    per_tile = KEY_TILE // SEL_LEN

    def per_head(x):
        return jnp.concatenate([x] * H, axis=1)

    qt = _q_transposed(q_ref)
    t = blk * Q + lax.broadcasted_iota(I32, (1, Q), 1)
    t_last = blk * Q + Q - 1
    slope_row = jnp.concatenate(
        [jnp.full((1, Q), slopes_ref[g * H + hh] * LOG2E, F32) for hh in range(H)], axis=1)
    key_in_tile = lax.broadcasted_iota(I32, (KEY_TILE, Q), 0)
    diag = blk // (KEY_TILE // Q)

    def reset():
        m_scr[...] = jnp.full_like(m_scr, M_INIT)
        l_scr[...] = jnp.zeros_like(l_scr)
        acc_scr[...] = jnp.zeros_like(acc_scr)

    def update(k_ref, vt_ref, tiles):
        scores = []
        for j, mask_fn in tiles:
            start = pl.multiple_of(j * KEY_TILE, KEY_TILE)
            pos = start + key_in_tile
            s = _dot(k_ref[0, pl.ds(start, KEY_TILE), :], qt) + per_head((pos - t_last).astype(F32)) * slope_row
            scores.append(jnp.where(per_head(mask_fn(pos)), s, NEG_INF))
        m_old = m_scr[...]
        m_new = m_old
        for s in scores:
            m_new = jnp.maximum(m_new, jnp.max(s, axis=0, keepdims=True))
        alpha = jnp.exp2(m_old - m_new)
        l = alpha * l_scr[...]
        acc = alpha * acc_scr[...]
        for (j, _), s in zip(tiles, scores):
            p = jnp.exp2(s - m_new)
            l = l + jnp.sum(p, axis=0, keepdims=True)
            acc = acc + _dot(vt_ref[0, j], p.astype(BF16))
        l_scr[...] = l
        acc_scr[...] = acc
        m_scr[...] = m_new

    def finish():
        return acc_scr[...] * (1.0 / jnp.maximum(l_scr[...], 1e-30))

    def picked(rows):
        spread = jnp.concatenate(
            [jnp.broadcast_to(rows[c:c + 1], (SEL_LEN, Q)) for c in range(per_tile)], axis=0)
        return lambda pos: jnp.where(pos <= t, spread, 0.0) > 0.5

    reset()
    act_base = ((b * pl.num_programs(1) + g) * nq + blk) * tiles_max

    def pair_body(jp, carry):
        @pl.when((act_ref[act_base + 2 * jp] > 0) | (act_ref[act_base + 2 * jp + 1] > 0))
        def _():
            rows = selt_ref[0, 0, 0, pl.ds(pl.multiple_of(jp * 2 * per_tile, 2 * per_tile), 2 * per_tile), :]
            update(ks_ref, vst_ref, [(2 * jp, picked(rows[0:per_tile])),
                                     (2 * jp + 1, picked(rows[per_tile:2 * per_tile]))])
        return carry

    lax.fori_loop(0, diag // 2 + 1, pair_body, 0)
    o_sel = finish()

    def in_window(exists):
        def mask(pos):
            dist = jnp.where(pos <= t, t - pos, WINDOW)
            return jnp.where(exists, dist, WINDOW) < WINDOW
        return mask

    reset()
    update(kw_ref, vwt_ref, [(jnp.maximum(diag - back, 0), in_window(diag - back >= 0))
                             for back in range(WINDOW // KEY_TILE, -1, -1)])
    o_win = finish()

    sig_scr[...] = _sigmoid(sm_ref[0].T)
    for hh in range(H):
        cols = slice(hh * Q, (hh + 1) * Q)
        gl = LANE_GATE + (g * H + hh) * 3
        o = (sig_scr[pl.ds(gl, 1), :] * oct_ref[0, 0, 0, hh * HEAD_DIM:(hh + 1) * HEAD_DIM, :]
             + sig_scr[pl.ds(gl + 1, 1), :] * o_sel[:, cols] + sig_scr[pl.ds(gl + 2, 1), :] * o_win[:, cols])
        on = o * lax.rsqrt(jnp.mean(o * o, axis=0, keepdims=True) + NORM_EPS) * ng_ref[...]
        o_ref[:, hh * HEAD_DIM:(hh + 1) * HEAD_DIM] = on.T.astype(o_ref.dtype)


def _nsa_attend(P, k_bf16, vt_tiles, selt, act_flat, oct, B, S, norm_g):
    G, Q, H = NSA_KV_GROUPS, Q_BLOCK, NSA_HPG
    nq = S // Q
    ns = S // SEL_LEN
    nt = S // KEY_TILE
    T = B * S
    slopes = jnp.asarray(_alibi_slopes(NSA_HEADS))
    ng = jnp.broadcast_to(norm_g[:, None], (HEAD_DIM, Q))
    per_blk = lambda b, g, i, *_: (b, g, i, 0, 0)

    def k_spec(base):
        return pl.BlockSpec((1, S, LANE), lambda b, g, i, *_: (base + g, b, 0))

    def vt_spec(base):
        return pl.BlockSpec((1, nt, HEAD_DIM, KEY_TILE), lambda b, g, i, *_: (base + g, b, 0, 0))

    grid_spec = pltpu.PrefetchScalarGridSpec(
        num_scalar_prefetch=2,
        grid=(B, G, nq),
        in_specs=[
            pl.BlockSpec((H, Q, LANE), lambda b, g, i, *_: (SLAB_NSA_Q // H + g, b * nq + i, 0)),
            k_spec(0), vt_spec(0), k_spec(G), vt_spec(G),
            pl.BlockSpec((1, 1, 1, ns, Q), per_blk),
            pl.BlockSpec((1, 1, 1, H * HEAD_DIM, Q), per_blk),
            pl.BlockSpec((1, Q, LANE), lambda b, g, i, *_: (SLAB_SMALL, b * nq + i, 0)),
            pl.BlockSpec((HEAD_DIM, Q), lambda b, g, i, *_: (0, 0)),
        ],
        out_specs=pl.BlockSpec((Q, H * HEAD_DIM), lambda b, g, i, *_: (b * nq + i, g)),
        scratch_shapes=[
            pltpu.VMEM((1, H * Q), F32), pltpu.VMEM((1, H * Q), F32),
            pltpu.VMEM((HEAD_DIM, H * Q), F32), pltpu.VMEM((LANE, Q), F32),
        ],
    )
    return pl.pallas_call(
        functools.partial(_nsa_attend_kernel, tiles_max=nt),
        out_shape=jax.ShapeDtypeStruct((T, NSA_HEADS * HEAD_DIM), BF16),
        grid_spec=grid_spec,
        compiler_params=_params(("parallel", "parallel", "arbitrary")),
        name="nsa_attend",
    )(slopes, act_flat, P, k_bf16, vt_tiles, k_bf16, vt_tiles, selt, oct, P, ng)


def _kv_prep_kernel(k_ref, v_ref, ko_ref, vo_ref):
    ko_ref[0] = k_ref[0].astype(ko_ref.dtype)
    for c in range(vo_ref.shape[1]):
        vo_ref[0, c] = v_ref[0, c * KEY_TILE:(c + 1) * KEY_TILE, :].T.astype(vo_ref.dtype)


def _attend_operands(P, B, S, tm=1024):
    G = NSA_KV_GROUPS
    T = B * S
    tm = min(tm, S)
    per_step = tm // KEY_TILE

    def slab(first):
        return lambda s, i: (first + (s // G) * (SLAB_K_WIN - SLAB_K_SEL) + s % G, i, 0)

    return pl.pallas_call(
        _kv_prep_kernel,
        out_shape=[jax.ShapeDtypeStruct((2 * G, T, LANE), BF16),
                   jax.ShapeDtypeStruct((2 * G, T // KEY_TILE, HEAD_DIM, KEY_TILE), BF16)],
        grid=(2 * G, T // tm),
        in_specs=[pl.BlockSpec((1, tm, LANE), slab(SLAB_K_SEL)), pl.BlockSpec((1, tm, LANE), slab(SLAB_V_SEL))],
        out_specs=[pl.BlockSpec((1, tm, LANE), lambda s, i: (s, i, 0)),
                   pl.BlockSpec((1, per_step, HEAD_DIM, KEY_TILE), lambda s, i: (s, i, 0, 0))],
        compiler_params=_params(("parallel", "parallel")),
        name="kv_prep",
    )(P, P)


def _active_tiles(act, S):
    return act[:, :, :, 0, :S // KEY_TILE].reshape(-1)


def _out_proj_kernel(ydn_ref, ynsa_ref, x_ref, wo_ref, g_ref, wrh_ref, wrl_ref, br_ref,
                     x1_ref, h2_ref, ri_ref, rw_ref, cnt_ref, carry_scr):
    tm = x_ref.shape[0]
    half = ynsa_ref.shape[1]

    @pl.when(pl.program_id(0) == 0)
    def _():
        carry_scr[...] = jnp.zeros_like(carry_scr)

    ydn = jnp.concatenate([ydn_ref[hh] for hh in range(ydn_ref.shape[0])], axis=1)
    x1 = x_ref[...] + _dot(ydn, wo_ref[0:half, :]) + _dot(ynsa_ref[...], wo_ref[half:, :])
    x1_ref[...] = x1
    h2 = x1 * lax.rsqrt(jnp.mean(x1 * x1, axis=-1, keepdims=True) + NORM_EPS) * g_ref[...]
    h2_ref[...] = h2

    hi = h2.astype(BF16)
    lo = (h2 - hi.astype(F32)).astype(BF16)
    lg = _dot(hi, wrh_ref[...]) + _dot(hi, wrl_ref[...]) + _dot(lo, wrh_ref[...])
    lane = lax.broadcasted_iota(I32, (1, LANE), 1)
    lanef = lane.astype(F32)
    biased = lg + br_ref[...]

    is_grp = lane < N_GROUPS
    gmax = jnp.max(jnp.where(is_grp, lg, NEG_INF), axis=-1, keepdims=True)
    eg = jnp.where(is_grp, jnp.exp(lg - gmax), 0.0)
    pg = eg / jnp.sum(eg, axis=-1, keepdims=True)
    gb = jnp.where(is_grp, biased, NEG_INF)
    g_sel = jnp.min(jnp.where(gb == jnp.max(gb, axis=-1, keepdims=True), lanef, float(LANE)),
                    axis=-1, keepdims=True)
    p_sel = _pick_lane(pg, lanef, g_sel)

    lo_lane = N_GROUPS + g_sel * EXPERTS_PER_GROUP
    in_grp = (lanef >= lo_lane) & (lanef < lo_lane + EXPERTS_PER_GROUP)
    v1 = jnp.where(in_grp, biased, NEG_INF)
    i1 = jnp.min(jnp.where(v1 == jnp.max(v1, axis=-1, keepdims=True), lanef, float(LANE)),
                 axis=-1, keepdims=True)
    v2 = jnp.where(lanef == i1, NEG_INF * 2, v1)
    i2 = jnp.min(jnp.where(v2 == jnp.max(v2, axis=-1, keepdims=True), lanef, float(LANE)),
                 axis=-1, keepdims=True)
    el1 = _pick_lane(lg, lanef, i1)
    el2 = _pick_lane(lg, lanef, i2)
    mx = jnp.maximum(el1, el2)
    e1 = jnp.exp(el1 - mx)
    e2 = jnp.exp(el2 - mx)
    rw_ref[...] = jnp.where(lane == 0, p_sel * (e1 / (e1 + e2)),
                            jnp.where(lane == 1, p_sel * (e2 / (e1 + e2)), 0.0))

    ex1 = i1 - N_GROUPS
    ex2 = i2 - N_GROUPS
    oh1 = lanef == ex1
    oh2 = lanef == ex2
    both = jnp.where(oh1 | oh2, 1.0, 0.0)
    ri = lax.broadcasted_iota(I32, (tm, tm), 0)
    ci = lax.broadcasted_iota(I32, (tm, tm), 1)
    before = _dot((ci < ri).astype(BF16), both.astype(BF16)) + carry_scr[...]
    r1 = jnp.sum(jnp.where(oh1, before, 0.0), axis=-1, keepdims=True)
    r2 = jnp.sum(jnp.where(oh2, before, 0.0), axis=-1, keepdims=True)
    packed = jnp.where(lane == 0, ex1, jnp.where(lane == 1, ex2,
                       jnp.where(lane == 2, r1, jnp.where(lane == 3, r2, 0.0))))
    ri_ref[...] = packed.astype(I32)
    total = carry_scr[...] + jnp.sum(both, axis=0, keepdims=True)
    carry_scr[...] = total
    cnt_ref[...] = jnp.broadcast_to(total, cnt_ref.shape).astype(I32)


def _out_proj(ydn, ynsa, x2d, wo, g, wr_hi, wr_lo, br, tm=256):
    T, D = x2d.shape
    half = ynsa.shape[1]
    row = lambda i: (i, 0)
    fixed = lambda i: (0, 0)
    return pl.pallas_call(
        _out_proj_kernel,
        out_shape=[
            jax.ShapeDtypeStruct((T, D), F32), jax.ShapeDtypeStruct((T, D), F32),
            jax.ShapeDtypeStruct((T, LANE), I32), jax.ShapeDtypeStruct((T, LANE), F32),
            jax.ShapeDtypeStruct((8, LANE), I32),
        ],
        grid=(T // tm,),
        in_specs=[
            pl.BlockSpec((ydn.shape[0], tm, LANE), lambda i: (0, i, 0)),
            pl.BlockSpec((tm, half), row), pl.BlockSpec((tm, D), row),
            pl.BlockSpec((D, D), fixed), pl.BlockSpec((1, D), fixed),
            pl.BlockSpec((D, LANE), fixed), pl.BlockSpec((D, LANE), fixed), pl.BlockSpec((1, LANE), fixed),
        ],
        out_specs=[
            pl.BlockSpec((tm, D), row), pl.BlockSpec((tm, D), row),
            pl.BlockSpec((tm, LANE), row), pl.BlockSpec((tm, LANE), row), pl.BlockSpec((8, LANE), fixed),
        ],
        scratch_shapes=[pltpu.VMEM((1, LANE), F32)],
        compiler_params=_params(("arbitrary",)),
        name="out_proj_router",
    )(ydn, ynsa, x2d, wo, g, wr_hi, wr_lo, br)


def _row_copy(src_hbm, row, dst, slot, sem):
    return pltpu.make_async_copy(src_hbm.at[pl.ds(row, 1)], dst.at[pl.ds(slot, 1)], sem)


GATHER_UNROLL = 8


def _gather_rows(src_hbm, idx_ref, base, dst, sem, n_rows):
    def issue(r, c):
        _row_copy(src_hbm, idx_ref[base + r], dst, r, sem).start()
        return c
    lax.fori_loop(0, n_rows, issue, 0, unroll=GATHER_UNROLL)


def _wait_rows(src_hbm, dst, sem):
    pltpu.make_async_copy(src_hbm.at[pl.ds(0, dst.shape[0])], dst, sem).wait()


def _moe_experts_kernel(be_ref, nu_ref, rt_ref, h_hbm, wg_ref, wu_ref, wd_ref, y_ref,
                        xs_a, xs_b, wg_scr, wu_scr, wd_scr, sem):
    i = pl.program_id(0)
    R = xs_a.shape[0]
    n_used = nu_ref[0]

    @pl.when((i == 0) & (n_used > 0))
    def _():
        _gather_rows(h_hbm, rt_ref, 0, xs_a, sem.at[0], R)

    @pl.when((i < n_used) & ((i == 0) | (be_ref[jnp.maximum(i - 1, 0)] != be_ref[i])))
    def _():
        wg_scr[...] = wg_ref[0].astype(BF16)
        wu_scr[...] = wu_ref[0].astype(BF16)
        wd_scr[...] = wd_ref[0].astype(BF16)

    def block(cur, nxt, s, prefetch):
        _wait_rows(h_hbm, cur, sem.at[s])
        if prefetch:
            for r in range(R):
                _row_copy(h_hbm, rt_ref[(i + 1) * R + r], nxt, r, sem.at[1 - s]).start()
        xb = cur[...].astype(BF16)
        hid = _silu(_dot(xb, wg_scr[...])) * _dot(xb, wu_scr[...])
        y_ref[...] = _dot(hid.astype(BF16), wd_scr[...])

    for s, (cur, nxt) in enumerate(((xs_a, xs_b), (xs_b, xs_a))):
        @pl.when((i % 2 == s) & (i + 1 < n_used))
        def _():
            block(cur, nxt, s, True)

        @pl.when((i % 2 == s) & (i + 1 == n_used))
        def _():
            block(cur, nxt, s, False)

    @pl.when(i >= n_used)
    def _():
        y_ref[...] = jnp.zeros_like(y_ref)


def _moe_experts(h2, block_expert, n_used, row_tok, w_gate, w_up, w_down):
    T, D = h2.shape
    R = MOE_ROWS
    n_blocks = block_expert.shape[0]
    DE = w_gate.shape[2]
    grid_spec = pltpu.PrefetchScalarGridSpec(
        num_scalar_prefetch=3,
        grid=(n_blocks,),
        in_specs=[
            pl.BlockSpec(memory_space=pl.ANY),
            pl.BlockSpec((1, D, DE), lambda i, be, nu, rt: (be[i], 0, 0)),
            pl.BlockSpec((1, D, DE), lambda i, be, nu, rt: (be[i], 0, 0)),
            pl.BlockSpec((1, DE, D), lambda i, be, nu, rt: (be[i], 0, 0)),
        ],
        out_specs=pl.BlockSpec((R, D), lambda i, be, nu, rt: (i, 0)),
        scratch_shapes=[
            pltpu.VMEM((R, D), F32), pltpu.VMEM((R, D), F32), pltpu.VMEM((D, DE), BF16),
            pltpu.VMEM((D, DE), BF16), pltpu.VMEM((DE, D), BF16), pltpu.SemaphoreType.DMA((2,)),
        ],
    )
    return pl.pallas_call(
        _moe_experts_kernel,
        out_shape=jax.ShapeDtypeStruct((n_blocks * R, D), F32),
        grid_spec=grid_spec,
        compiler_params=_params(("arbitrary",)),
        name="moe_experts",
    )(block_expert, n_used, row_tok, h2, w_gate, w_up, w_down)


def _moe_combine_kernel(dest_ref, x1_ref, rw_ref, g_ref, ys_hbm, o_ref, buf_a, buf_b, sem):
    i = pl.program_id(0)
    n_steps = pl.num_programs(0)
    tm = x1_ref.shape[0]
    T = tm * n_steps

    @pl.when(i == 0)
    def _():
        for k in range(2):
            _gather_rows(ys_hbm, dest_ref, k * T, buf_a.at[k], sem.at[0], tm)

    def step(cur, nxt, s, prefetch):
        for k in range(2):
            _wait_rows(ys_hbm, cur.at[k], sem.at[s])
        if prefetch:
            for k in range(2):
                for r in range(tm):
                    _row_copy(ys_hbm, dest_ref[k * T + (i + 1) * tm + r], nxt.at[k], r, sem.at[1 - s]).start()
        rw = rw_ref[...]
        x2 = x1_ref[...] + cur[0] * rw[:, 0:1] + cur[1] * rw[:, 1:2]
        o_ref[...] = x2 * lax.rsqrt(jnp.mean(x2 * x2, axis=-1, keepdims=True) + NORM_EPS) * g_ref[...]

    for s, (cur, nxt) in enumerate(((buf_a, buf_b), (buf_b, buf_a))):
        @pl.when((i % 2 == s) & (i + 1 < n_steps))
        def _():
            step(cur, nxt, s, True)

        @pl.when((i % 2 == s) & (i + 1 == n_steps))
        def _():
            step(cur, nxt, s, False)


def _moe_combine(dest_flat, x1, rw, g, ys, tm=128):
    T, D = x1.shape
    grid_spec = pltpu.PrefetchScalarGridSpec(
        num_scalar_prefetch=1,
        grid=(T // tm,),
        in_specs=[
            pl.BlockSpec((tm, D), lambda i, d: (i, 0)),
            pl.BlockSpec((tm, LANE), lambda i, d: (i, 0)),
            pl.BlockSpec((1, D), lambda i, d: (0, 0)),
            pl.BlockSpec(memory_space=pl.ANY),
        ],
        out_specs=pl.BlockSpec((tm, D), lambda i, d: (i, 0)),
        scratch_shapes=[pltpu.VMEM((2, tm, D), F32), pltpu.VMEM((2, tm, D), F32),
                        pltpu.SemaphoreType.DMA((2,))],
    )
    return pl.pallas_call(
        _moe_combine_kernel,
        out_shape=jax.ShapeDtypeStruct((T, D), F32),
        grid_spec=grid_spec,
        compiler_params=_params(("arbitrary",)),
        name="moe_combine",
    )(dest_flat, x1, rw, g, ys)


def _slab_weights(w_in):
    D = w_in.shape[0]
    n_dn = 4 * DN_HEADS * HEAD_DIM
    n_small_a = 2 * DN_HEADS
    n_nsa = NSA_HEADS * HEAD_DIM + 6 * NSA_KV_GROUPS * HEAD_DIM
    c0 = n_dn + n_small_a
    c1 = c0 + n_nsa
    n_gate = 3 * NSA_HEADS
    pad = N_SLABS * LANE - (n_dn + n_nsa + n_small_a + n_gate)
    return jnp.concatenate(
        [w_in[:, :n_dn], w_in[:, c0:c1], w_in[:, n_dn:c0], w_in[:, c1:c1 + n_gate],
         jnp.zeros((D, pad), w_in.dtype)], axis=1).astype(BF16)


def _layer(x, norm_mix_g, w_in, conv_w, dt_bias, a_log, dn_norm_g, cmp_pos_k, cmp_w1_k, cmp_w2_k,
           cmp_pos_v, cmp_w1_v, cmp_w2_v, nsa_norm_g, w_out, norm_ffn_g, w_group, b_group,
           w_router, b_router, w_gate, w_up, w_down, norm_final_g):
    B, S, D = x.shape
    T = B * S
    x2d = x.reshape(T, D)

    P = _in_proj(x2d, norm_mix_g.reshape(1, D), _slab_weights(w_in))
    ydn = _deltanet(P, B, S, conv_w, dt_bias, a_log, dn_norm_g)

    kvc, kvct = _compress(P, B, S, jnp.stack([cmp_pos_k, cmp_pos_v]), jnp.stack([cmp_w1_k, cmp_w1_v]),
                          jnp.stack([cmp_w2_k, cmp_w2_v]))
    oct, selt, act = _nsa_select(P, kvc, kvct, B, S)
    ynsa = _nsa_attend(P, *_attend_operands(P, B, S), selt, _active_tiles(act, S), oct, B, S, nsa_norm_g)

    wr = jnp.zeros((D, LANE), F32).at[:, :N_GROUPS].set(w_group).at[:, N_GROUPS:N_GROUPS + N_EXPERTS].set(w_router)
    wr_hi = wr.astype(BF16)
    wr_lo = (wr - wr_hi.astype(F32)).astype(BF16)
    br = jnp.zeros((1, LANE), F32).at[0, :N_GROUPS].set(b_group).at[0, N_GROUPS:N_GROUPS + N_EXPERTS].set(b_router)
    x1, h2, ri, rw, cnt = _out_proj(ydn, ynsa, x2d, w_out.astype(BF16), norm_ffn_g.reshape(1, D),
                                    wr_hi, wr_lo, br)

    R = MOE_ROWS
    counts = cnt[0, :N_EXPERTS]
    padded = (counts + R - 1) // R * R
    pends = jnp.cumsum(padded)
    pstarts = pends - padded
    first_row = jnp.sum(jnp.where(ri[:, 0:2, None] == jnp.arange(N_EXPERTS, dtype=I32), pstarts, 0), axis=-1)
    dest = first_row + ri[:, 2:4]
    dest_flat = dest.T.reshape(-1).astype(I32)
    n_blocks = (2 * T + R - 1) // R + N_EXPERTS
    block_start = jnp.arange(n_blocks, dtype=I32) * R
    block_expert = jnp.minimum(jnp.sum(block_start[:, None] >= pends[None, :], axis=1),
                               N_EXPERTS - 1).astype(I32)
    n_used = (pends[-1:] // R).astype(I32)
    row_tok = jnp.zeros((n_blocks * R,), I32).at[dest_flat].set(jnp.tile(jnp.arange(T, dtype=I32), 2))

    ys = _moe_experts(h2, block_expert, n_used, row_tok, w_gate, w_up, w_down)
    out = _moe_combine(dest_flat, x1, rw, norm_final_g.reshape(1, D), ys)
    return out.reshape(B, S, D)


def kernel(x, norm_mix_g, w_in, conv_w, dt_bias, a_log, dn_norm_g, cmp_pos_k, cmp_w1_k, cmp_w2_k,
           cmp_pos_v, cmp_w1_v, cmp_w2_v, nsa_norm_g, w_out, norm_ffn_g, w_group, b_group,
           w_router, b_router, w_gate, w_up, w_down, norm_final_g):
    assert w_in.shape[0] == 1, "one layer"
    return _layer(x, norm_mix_g[0], w_in[0], conv_w[0], dt_bias[0], a_log[0], dn_norm_g[0],
                  cmp_pos_k[0], cmp_w1_k[0], cmp_w2_k[0], cmp_pos_v[0], cmp_w1_v[0], cmp_w2_v[0],
                  nsa_norm_g[0], w_out[0], norm_ffn_g[0], w_group[0], b_group[0], w_router[0],
                  b_router[0], w_gate[0], w_up[0], w_down[0], norm_final_g)
```

```python
import functools
import math

import numpy as np
import jax
import jax.numpy as jnp
from jax import lax
from jax.experimental import pallas as pl
from jax.experimental.pallas import tpu as pltpu

F32 = jnp.float32
BF16 = jnp.bfloat16
I32 = jnp.int32

LANE = 128
HEAD_DIM = 128
DN_HEADS = 8
NSA_HEADS = 8
NSA_KV_GROUPS = 2
NSA_HPG = NSA_HEADS // NSA_KV_GROUPS
CONV_WIDTH = 4
DN_CHUNK = 64
CMP_LEN = 32
CMP_STRIDE = 16
SEL_LEN = 64
SEL_COUNT = 16
WINDOW = 512
Q_BLOCK = 128
N_GROUPS = 8
EXPERTS_PER_GROUP = 8
N_EXPERTS = N_GROUPS * EXPERTS_PER_GROUP
NORM_EPS = 1e-6
NEG_INF = -1e30
SEL_FORCE = 1e9
PICKED = -3e38

SLAB_DN_Q, SLAB_DN_K, SLAB_DN_V, SLAB_DN_Z = 0, 8, 16, 24
SLAB_NSA_Q = 32
SLAB_K_CMP, SLAB_V_CMP, SLAB_K_SEL, SLAB_V_SEL, SLAB_K_WIN, SLAB_V_WIN = 40, 42, 44, 46, 48, 50
SLAB_SMALL = 52
N_SLABS = 54
LANE_DN_B, LANE_DN_A, LANE_GATE = 0, 8, 16

KEY_TILE = 256
MOE_ROWS = 256

VMEM_LIMIT = 56 * 1024 * 1024


def _params(sem, vmem=VMEM_LIMIT):
    return pltpu.CompilerParams(dimension_semantics=sem, vmem_limit_bytes=vmem)


def _dot(a, b):
    return jnp.dot(a, b, preferred_element_type=F32)


def _dot_nt(a, b):
    return lax.dot_general(a, b, (((1,), (1,)), ((), ())), preferred_element_type=F32)


def _dot_ones(ones_mask, x):
    n = x.shape[1]
    hi = x.astype(BF16)
    rest = x - hi.astype(F32)
    mid = rest.astype(BF16)
    lo = (rest - mid.astype(F32)).astype(BF16)
    r = _dot(ones_mask.astype(BF16), jnp.concatenate([hi, mid, lo], axis=1))
    return r[:, 0:n] + r[:, n:2 * n] + r[:, 2 * n:3 * n]


def _sigmoid(x):
    return 1.0 / (1.0 + jnp.exp(-x))


def _silu(x):
    return x * _sigmoid(x)


def _pick_lane(x, lane_ids, idx):
    return jnp.sum(jnp.where(lane_ids == idx, x, 0.0), axis=1, keepdims=True)


def _in_proj_kernel(x_ref, g_ref, w_ref, o_ref, h_scr):
    @pl.when(pl.program_id(1) == 0)
    def _():
        xf = x_ref[...]
        ms = jnp.mean(xf * xf, axis=-1, keepdims=True)
        h_scr[...] = (xf * lax.rsqrt(ms + NORM_EPS) * g_ref[...]).astype(BF16)

    acc = _dot(h_scr[...], w_ref[...])
    for k in range(o_ref.shape[0]):
        o_ref[k] = acc[:, k * LANE:(k + 1) * LANE]


def _in_proj(x2d, g, w_slabs, tm=1024, tn=768):
    T, D = x2d.shape
    NP = w_slabs.shape[1]
    ns = tn // LANE
    return pl.pallas_call(
        _in_proj_kernel,
        out_shape=jax.ShapeDtypeStruct((NP // LANE, T, LANE), F32),
        grid=(T // tm, NP // tn),
        in_specs=[
            pl.BlockSpec((tm, D), lambda i, j: (i, 0)),
            pl.BlockSpec((1, D), lambda i, j: (0, 0)),
            pl.BlockSpec((D, tn), lambda i, j: (0, j)),
        ],
        out_specs=pl.BlockSpec((ns, tm, LANE), lambda i, j: (j, i, 0)),
        scratch_shapes=[pltpu.VMEM((tm, D), BF16)],
        compiler_params=_params(("parallel", "arbitrary")),
        name="in_proj",
    )(x2d, g, w_slabs)


DN_HEADS_PER_ITER = 8


def _deltanet_kernel(p_ref, sm_ref, cw_ref, alog_ref, dtb_ref, ng_ref, o_ref,
                     ext_scr, s_scr, gc_scr, gl_scr, gct_scr, beta_scr):
    L = p_ref.shape[1]
    C = DN_CHUNK
    H = DN_HEADS

    @pl.when(pl.program_id(1) == 0)
    def _():
        ext_scr[:, 0:8, :] = jnp.zeros((3 * H, 8, LANE), F32)
        s_scr[...] = jnp.zeros_like(s_scr)

    sm = sm_ref[0]
    lane = lax.broadcasted_iota(I32, (1, LANE), 1)
    xa = sm + dtb_ref[...]
    softplus = jnp.maximum(xa, 0.0) + jnp.log1p(jnp.exp(-jnp.abs(xa)))
    ld = -jnp.exp(alog_ref[...]) * softplus
    ri = lax.broadcasted_iota(I32, (L, L), 0)
    ci = lax.broadcasted_iota(I32, (L, L), 1)
    same = (ri // C) == (ci // C)
    causal = same & (ci <= ri)
    gc_slab = _dot_ones(jnp.where(causal, 1.0, 0.0), ld)
    gc_scr[...] = gc_slab
    gct_scr[...] = gc_slab.T
    gl_scr[...] = _dot_ones(jnp.where(same, 1.0, 0.0), ld)
    beta_scr[...] = _sigmoid(sm)

    def conv_silu(slab, h):
        idx = slab + h
        ext_scr[idx, 8:, :] = p_ref[idx]
        w = cw_ref[idx]
        acc = ext_scr[idx, pl.ds(8 - (CONV_WIDTH - 1), L), :] * w[0:1]
        for j in range(1, CONV_WIDTH):
            acc = acc + ext_scr[idx, pl.ds(8 - (CONV_WIDTH - 1) + j, L), :] * w[j:j + 1]
        ext_scr[idx, 0:8, :] = ext_scr[idx, L:L + 8, :]
        return _silu(acc)

    def l2n(x):
        return x * lax.rsqrt(jnp.sum(x * x, axis=-1, keepdims=True) + NORM_EPS)

    def head_group(hg, carry):
        hs = [hg * DN_HEADS_PER_ITER + dh for dh in range(DN_HEADS_PER_ITER)]
        n = range(len(hs))
        qs = [l2n(conv_silu(SLAB_DN_Q, h)) * (HEAD_DIM ** -0.5) for h in hs]
        ks = [l2n(conv_silu(SLAB_DN_K, h)) for h in hs]
        vs = [conv_silu(SLAB_DN_V, h) for h in hs]
        betas = [_pick_lane(beta_scr[...], lane, LANE_DN_B + h) for h in hs]
        gcols = [_pick_lane(gc_scr[...], lane, LANE_DN_A + h) for h in hs]
        glasts = [_pick_lane(gl_scr[...], lane, LANE_DN_A + h) for h in hs]
        grows = [gct_scr[pl.ds(LANE_DN_A + h, 1), :] for h in hs]
        states = [s_scr[h] for h in hs]
        zs = [p_ref[SLAB_DN_Z + h] for h in hs]

        decays = [jnp.where(causal, jnp.exp(jnp.where(causal, gcols[i] - grows[i], 0.0)), 0.0) for i in n]
        egcs = [jnp.exp(gcols[i]) for i in n]
        kbs = [ks[i] * betas[i] for i in n]
        kts = [ks[i].astype(BF16) for i in n]
        lms = [jnp.where(ci < ri, _dot_nt(kbs[i].astype(BF16), kts[i]) * decays[i], 0.0) for i in n]
        row_c = lax.broadcasted_iota(I32, (C, L), 0)
        lane_c = lax.broadcasted_iota(I32, (C, L), 1)

        def side_by_side(m):
            out = jnp.where(lane_c // C == 0, m[0:C], 0.0)
            for c in range(1, L // C):
                out = out + jnp.where(lane_c // C == c, m[c * C:(c + 1) * C], 0.0)
            return out

        def block_diag(m):
            return jnp.where(same, jnp.concatenate([m] * (L // C), axis=0), 0.0).astype(BF16)

        pws = [side_by_side(lms[i]) for i in n]
        tinvs = [jnp.where(row_c == lane_c % C, 1.0, 0.0) - pws[i] for i in n]
        for _ in range(int(math.log2(C)) - 1):
            pws = [_dot(pws[i].astype(BF16), block_diag(pws[i])) for i in n]
            tinvs = [tinvs[i] + _dot(tinvs[i].astype(BF16), block_diag(pws[i])) for i in n]
        tbs = [block_diag(tinvs[i]) for i in n]
        us = [_dot(tbs[i], (vs[i] * betas[i]).astype(BF16)) for i in n]
        ws = [_dot(tbs[i], (kbs[i] * egcs[i]).astype(BF16)) for i in n]
        qks = [_dot_nt(qs[i].astype(BF16), kts[i]) * decays[i] for i in n]
        q_decs = [(qs[i] * egcs[i]).astype(BF16) for i in n]
        k_dec_ts = [(ks[i] * jnp.exp(glasts[i] - gcols[i])).T for i in n]

        outs = [[] for _ in n]
        for c in range(L // C):
            sl = slice(c * C, (c + 1) * C)
            sbs = [states[i].astype(BF16) for i in n]
            vnbs = [(us[i][sl] - _dot(ws[i][sl].astype(BF16), sbs[i])).astype(BF16) for i in n]
            for i in n:
                outs[i].append(_dot(q_decs[i][sl], sbs[i]) + _dot(qks[i][sl, sl].astype(BF16), vnbs[i]))
            states = [states[i] * jnp.exp(glasts[i][c * C:c * C + 1, :])
                      + _dot(k_dec_ts[i][:, sl].astype(BF16), vnbs[i]) for i in n]

        for i, h in enumerate(hs):
            s_scr[h] = states[i]
            o = jnp.concatenate(outs[i], axis=0)
            on = o * lax.rsqrt(jnp.mean(o * o, axis=-1, keepdims=True) + NORM_EPS) * ng_ref[...]
            o_ref[h] = (on * _silu(zs[i])).astype(o_ref.dtype)
        return carry

    lax.fori_loop(0, H // DN_HEADS_PER_ITER, head_group, 0)


def _deltanet(P, B, S, conv_w, dt_bias, a_log, norm_g, L=256):
    T = B * S
    nL = S // L
    H = DN_HEADS
    cw = conv_w.reshape(CONV_WIDTH, 3 * H, LANE).transpose(1, 0, 2)
    alog = jnp.zeros((1, LANE), F32).at[0, LANE_DN_A:LANE_DN_A + H].set(a_log)
    dtb = jnp.zeros((1, LANE), F32).at[0, LANE_DN_A:LANE_DN_A + H].set(dt_bias)
    vec = pl.BlockSpec((1, LANE), lambda b, i: (0, 0))
    return pl.pallas_call(
        _deltanet_kernel,
        out_shape=jax.ShapeDtypeStruct((H, T, LANE), BF16),
        grid=(B, nL),
        in_specs=[
            pl.BlockSpec((4 * H, L, LANE), lambda b, i: (0, b * nL + i, 0)),
            pl.BlockSpec((1, L, LANE), lambda b, i: (SLAB_SMALL, b * nL + i, 0)),
            pl.BlockSpec((3 * H, CONV_WIDTH, LANE), lambda b, i: (0, 0, 0)),
            vec, vec, vec,
        ],
        out_specs=pl.BlockSpec((H, L, LANE), lambda b, i: (0, b * nL + i, 0)),
        scratch_shapes=[
            pltpu.VMEM((3 * H, L + 8, LANE), F32), pltpu.VMEM((H, HEAD_DIM, HEAD_DIM), F32),
            pltpu.VMEM((L, LANE), F32), pltpu.VMEM((L, LANE), F32), pltpu.VMEM((LANE, L), F32),
            pltpu.VMEM((L, LANE), F32),
        ],
        compiler_params=_params(("parallel", "arbitrary")),
        name="deltanet",
    )(P, P, cw, alog, dtb, norm_g.reshape(1, LANE))


def _gelu_tanh(x):
    return 0.5 * x * (1.0 + jnp.tanh(math.sqrt(2.0 / math.pi) * (x + 0.044715 * (x * x * x))))


def _compress_kernel(x_ref, pos_ref, w1_ref, w2_ref, o_ref, ot_ref):
    n = o_ref.shape[3]
    half = CMP_LEN // 2
    y1 = jnp.zeros((n, HEAD_DIM), F32)
    y2 = jnp.zeros((n, HEAD_DIM), F32)
    for l in range(half):
        xl = x_ref[0, pl.ds(l, n, stride=CMP_STRIDE), :]
        y1 = y1 + _dot((xl + pos_ref[0, l:l + 1, :]).astype(BF16), w1_ref[0, l].astype(BF16))
        y2 = y2 + _dot((xl + pos_ref[0, half + l:half + l + 1, :]).astype(BF16),
                       w1_ref[0, half + l].astype(BF16))
    hid = _gelu_tanh(y1 + pltpu.roll(y2, n - 1, 0))
    row = lax.broadcasted_iota(I32, (n, 1), 0)
    hb = jnp.where(row < n - 1, hid, 0.0).astype(BF16)
    w2 = w2_ref[0]
    o_ref[0, 0, 0] = _dot(hb, w2.astype(BF16)).astype(o_ref.dtype)
    ot_ref[0, 0, 0] = _dot_nt(w2.T.astype(BF16), hb).astype(ot_ref.dtype)


def _compress(P, B, S, pos, w1, w2):
    n = S // CMP_STRIDE
    G = NSA_KV_GROUPS
    return pl.pallas_call(
        _compress_kernel,
        out_shape=[jax.ShapeDtypeStruct((2, B, G, n, HEAD_DIM), BF16),
                   jax.ShapeDtypeStruct((2, B, G, HEAD_DIM, n), BF16)],
        grid=(2, B, G),
        in_specs=[
            pl.BlockSpec((1, S, LANE), lambda c, b, g: (SLAB_K_CMP + 2 * c + g, b, 0)),
            pl.BlockSpec((1, CMP_LEN, HEAD_DIM), lambda c, b, g: (c, 0, 0)),
            pl.BlockSpec((1, CMP_LEN, HEAD_DIM, HEAD_DIM), lambda c, b, g: (c, 0, 0, 0)),
            pl.BlockSpec((1, HEAD_DIM, HEAD_DIM), lambda c, b, g: (c, 0, 0)),
        ],
        out_specs=[pl.BlockSpec((1, 1, 1, n, HEAD_DIM), lambda c, b, g: (c, b, g, 0, 0)),
                   pl.BlockSpec((1, 1, 1, HEAD_DIM, n), lambda c, b, g: (c, b, g, 0, 0))],
        compiler_params=_params(("parallel", "parallel", "parallel")),
        name="compress",
    )(P, pos, w1, w2)


LOG2E = math.log2(math.e)
M_INIT = -1e29


def _q_transposed(q_ref):
    c1 = (HEAD_DIM ** -0.5) * LOG2E
    return jnp.concatenate([(q_ref[hh] * c1).T for hh in range(q_ref.shape[0])], axis=1).astype(BF16)


def _nsa_select_kernel(slopes_ref, q_ref, kc_ref, vct_ref, aggt_ref, grp_ref, oct_ref, selt_ref, act_ref):
    g = pl.program_id(1)
    blk = pl.program_id(2)
    Q, H = Q_BLOCK, NSA_HPG
    n = kc_ref.shape[3]
    ns = selt_ref.shape[3]
    st = _dot(kc_ref[0, 0, 0], _q_transposed(q_ref))
    vct = vct_ref[0, 0, 0]

    t = blk * Q + lax.broadcasted_iota(I32, (1, Q), 1)
    cpos = lax.broadcasted_iota(I32, (n, Q), 0) * CMP_STRIDE + (CMP_LEN - 1)
    valid = cpos <= t
    rel = (cpos - (blk * Q + Q - 1)).astype(F32)

    psum = jnp.zeros((n, Q), F32)
    for hh in range(H):
        sh = jnp.where(valid, st[:, hh * Q:(hh + 1) * Q] + rel * (slopes_ref[g * H + hh] * LOG2E), NEG_INF)
        e = jnp.where(valid, jnp.exp2(sh - jnp.max(sh, axis=0, keepdims=True)), 0.0)
        p = e * (1.0 / jnp.maximum(jnp.sum(e, axis=0, keepdims=True), 1e-30))
        oct_ref[0, 0, 0, hh * HEAD_DIM:(hh + 1) * HEAD_DIM, :] = _dot(vct, p.astype(BF16))
        psum = psum + p

    hi = psum.astype(BF16)
    lo = (psum - hi.astype(F32)).astype(BF16)
    imp = _dot(aggt_ref[...], hi) + _dot(aggt_ref[...], lo)

    sid = lax.broadcasted_iota(I32, (ns, Q), 0)
    sidf = sid.astype(F32)
    cur = t // SEL_LEN
    forced = (sid == 0) | (sid == cur) | (sid == cur - 1)
    valid_s = sid * SEL_LEN <= t
    work = jnp.where(forced, SEL_FORCE, jnp.where(valid_s, imp, NEG_INF))
    picked = jnp.zeros((ns, Q), F32)
    for _ in range(min(SEL_COUNT, ns)):
        m_w = jnp.max(work, axis=0, keepdims=True)
        first = jnp.min(jnp.where(work == m_w, sidf, float(ns)), axis=0, keepdims=True)
        hit = sidf == first
        picked = jnp.where(hit, 1.0, picked)
        work = jnp.where(hit, PICKED, work)
    selt = jnp.where(valid_s, picked, 0.0)
    selt_ref[0, 0, 0] = selt
    per_block = _dot_nt(jnp.ones((8, Q), BF16), selt.astype(BF16))
    tiles = _dot(per_block.astype(BF16), grp_ref[...])
    act_ref[0, 0, 0] = (tiles > 0.5).astype(I32)


def _cmp_to_sel_matrix(n_cmp_padded, n_sel):
    r = SEL_LEN // CMP_STRIDE
    c = CMP_LEN // CMP_STRIDE
    off = np.arange(n_cmp_padded)[:, None] - r * np.arange(n_sel)[None, :] + (c - 1)
    cnt = np.minimum(c - 1, off) - np.maximum(0, off - r + 1) + 1
    return np.clip(cnt, 0, None).astype(np.float32)


def _alibi_slopes(n_heads):
    return (2.0 ** (-8.0 * (np.arange(n_heads) + 1) / n_heads)).astype(np.float32)


def _nsa_select(P, kvc, kvct, B, S):
    G, Q, H = NSA_KV_GROUPS, Q_BLOCK, NSA_HPG
    nq = S // Q
    n = S // CMP_STRIDE
    ns = S // SEL_LEN
    aggt = jnp.asarray(_cmp_to_sel_matrix(n, ns).T, BF16)
    per_tile = KEY_TILE // SEL_LEN
    grp = jnp.asarray((np.arange(ns)[:, None] // per_tile) == np.arange(LANE)[None, :], BF16)
    slopes = jnp.asarray(_alibi_slopes(NSA_HEADS))
    per_blk = lambda b, g, i, s_: (b, g, i, 0, 0)
    grid_spec = pltpu.PrefetchScalarGridSpec(
        num_scalar_prefetch=1,
        grid=(B, G, nq),
        in_specs=[
            pl.BlockSpec((H, Q, LANE), lambda b, g, i, s_: (SLAB_NSA_Q // H + g, b * nq + i, 0)),
            pl.BlockSpec((1, 1, 1, n, HEAD_DIM), lambda b, g, i, s_: (0, b, g, 0, 0)),
            pl.BlockSpec((1, 1, 1, HEAD_DIM, n), lambda b, g, i, s_: (1, b, g, 0, 0)),
            pl.BlockSpec((ns, n), lambda b, g, i, s_: (0, 0)),
            pl.BlockSpec((ns, LANE), lambda b, g, i, s_: (0, 0)),
        ],
        out_specs=[
            pl.BlockSpec((1, 1, 1, H * HEAD_DIM, Q), per_blk),
            pl.BlockSpec((1, 1, 1, ns, Q), per_blk),
            pl.BlockSpec((1, 1, 1, 8, LANE), per_blk),
        ],
    )
    return pl.pallas_call(
        _nsa_select_kernel,
        out_shape=[
            jax.ShapeDtypeStruct((B, G, nq, H * HEAD_DIM, Q), F32),
            jax.ShapeDtypeStruct((B, G, nq, ns, Q), F32),
            jax.ShapeDtypeStruct((B, G, nq, 8, LANE), I32),
        ],
        grid_spec=grid_spec,
        compiler_params=_params(("parallel", "parallel", "parallel")),
        name="nsa_select",
    )(slopes, P, kvc, kvct, aggt, grp)


def _nsa_attend_kernel(slopes_ref, act_ref, q_ref, ks_ref, vst_ref, kw_ref, vwt_ref, selt_ref, oct_ref,
                       sm_ref, ng_ref, o_ref, m_scr, l_scr, acc_scr, sig_scr, *, tiles_max):
    b = pl.program_id(0)
    g = pl.program_id(1)
    blk = pl.program_id(2)
    nq = pl.num_programs(2)
    Q, H = Q_BLOCK, NSA_HPG
    per_tile = KEY_TILE // SEL_LEN

    def per_head(x):
        return jnp.concatenate([x] * H, axis=1)

    qt = _q_transposed(q_ref)
    t = blk * Q + lax.broadcasted_iota(I32, (1, Q), 1)
    t_last = blk * Q + Q - 1
    slope_row = jnp.concatenate(
        [jnp.full((1, Q), slopes_ref[g * H + hh] * LOG2E, F32) for hh in range(H)], axis=1)
    key_in_tile = lax.broadcasted_iota(I32, (KEY_TILE, Q), 0)
    diag = blk // (KEY_TILE // Q)

    def reset():
        m_scr[...] = jnp.full_like(m_scr, M_INIT)
        l_scr[...] = jnp.zeros_like(l_scr)
        acc_scr[...] = jnp.zeros_like(acc_scr)

    def update(k_ref, vt_ref, tiles):
        scores = []
        for j, mask_fn in tiles:
            start = pl.multiple_of(j * KEY_TILE, KEY_TILE)
            pos = start + key_in_tile
            s = _dot(k_ref[0, pl.ds(start, KEY_TILE), :], qt) + per_head((pos - t_last).astype(F32)) * slope_row
            scores.append(jnp.where(per_head(mask_fn(pos)), s, NEG_INF))
        m_old = m_scr[...]
        m_new = m_old
        for s in scores:
            m_new = jnp.maximum(m_new, jnp.max(s, axis=0, keepdims=True))
        alpha = jnp.exp2(m_old - m_new)
        l = alpha * l_scr[...]
        acc = alpha * acc_scr[...]
        for (j, _), s in zip(tiles, scores):
            p = jnp.exp2(s - m_new)
            l = l + jnp.sum(p, axis=0, keepdims=True)
            acc = acc + _dot(vt_ref[0, j], p.astype(BF16))
        l_scr[...] = l
        acc_scr[...] = acc
        m_scr[...] = m_new

    def finish():
        return acc_scr[...] * (1.0 / jnp.maximum(l_scr[...], 1e-30))

    def picked(rows):
        spread = jnp.concatenate(
            [jnp.broadcast_to(rows[c:c + 1], (SEL_LEN, Q)) for c in range(per_tile)], axis=0)
        return lambda pos: jnp.where(pos <= t, spread, 0.0) > 0.5

    reset()
    act_base = ((b * pl.num_programs(1) + g) * nq + blk) * tiles_max

    def pair_body(jp, carry):
        @pl.when((act_ref[act_base + 2 * jp] > 0) | (act_ref[act_base + 2 * jp + 1] > 0))
        def _():
            rows = selt_ref[0, 0, 0, pl.ds(pl.multiple_of(jp * 2 * per_tile, 2 * per_tile), 2 * per_tile), :]
            update(ks_ref, vst_ref, [(2 * jp, picked(rows[0:per_tile])),
                                     (2 * jp + 1, picked(rows[per_tile:2 * per_tile]))])
        return carry

    lax.fori_loop(0, diag // 2 + 1, pair_body, 0)
    o_sel = finish()

    def in_window(exists):
        def mask(pos):
            dist = jnp.where(pos <= t, t - pos, WINDOW)
            return jnp.where(exists, dist, WINDOW) < WINDOW
        return mask

    reset()
    update(kw_ref, vwt_ref, [(jnp.maximum(diag - back, 0), in_window(diag - back >= 0))
                             for back in range(WINDOW // KEY_TILE, -1, -1)])
    o_win = finish()

    sig_scr[...] = _sigmoid(sm_ref[0].T)
    for hh in range(H):
        cols = slice(hh * Q, (hh + 1) * Q)
        gl = LANE_GATE + (g * H + hh) * 3
        o = (sig_scr[pl.ds(gl, 1), :] * oct_ref[0, 0, 0, hh * HEAD_DIM:(hh + 1) * HEAD_DIM, :]
             + sig_scr[pl.ds(gl + 1, 1), :] * o_sel[:, cols] + sig_scr[pl.ds(gl + 2, 1), :] * o_win[:, cols])
        on = o * lax.rsqrt(jnp.mean(o * o, axis=0, keepdims=True) + NORM_EPS) * ng_ref[...]
        o_ref[:, hh * HEAD_DIM:(hh + 1) * HEAD_DIM] = on.T.astype(o_ref.dtype)


def _nsa_attend(P, k_bf16, vt_tiles, selt, act_flat, oct, B, S, norm_g):
    G, Q, H = NSA_KV_GROUPS, Q_BLOCK, NSA_HPG
    nq = S // Q
    ns = S // SEL_LEN
    nt = S // KEY_TILE
    T = B * S
    slopes = jnp.asarray(_alibi_slopes(NSA_HEADS))
    ng = jnp.broadcast_to(norm_g[:, None], (HEAD_DIM, Q))
    per_blk = lambda b, g, i, *_: (b, g, i, 0, 0)

    def k_spec(base):
        return pl.BlockSpec((1, S, LANE), lambda b, g, i, *_: (base + g, b, 0))

    def vt_spec(base):
        return pl.BlockSpec((1, nt, HEAD_DIM, KEY_TILE), lambda b, g, i, *_: (base + g, b, 0, 0))

    grid_spec = pltpu.PrefetchScalarGridSpec(
        num_scalar_prefetch=2,
        grid=(B, G, nq),
        in_specs=[
            pl.BlockSpec((H, Q, LANE), lambda b, g, i, *_: (SLAB_NSA_Q // H + g, b * nq + i, 0)),
            k_spec(0), vt_spec(0), k_spec(G), vt_spec(G),
            pl.BlockSpec((1, 1, 1, ns, Q), per_blk),
            pl.BlockSpec((1, 1, 1, H * HEAD_DIM, Q), per_blk),
            pl.BlockSpec((1, Q, LANE), lambda b, g, i, *_: (SLAB_SMALL, b * nq + i, 0)),
            pl.BlockSpec((HEAD_DIM, Q), lambda b, g, i, *_: (0, 0)),
        ],
        out_specs=pl.BlockSpec((Q, H * HEAD_DIM), lambda b, g, i, *_: (b * nq + i, g)),
        scratch_shapes=[
            pltpu.VMEM((1, H * Q), F32), pltpu.VMEM((1, H * Q), F32),
            pltpu.VMEM((HEAD_DIM, H * Q), F32), pltpu.VMEM((LANE, Q), F32),
        ],
    )
    return pl.pallas_call(
        functools.partial(_nsa_attend_kernel, tiles_max=nt),
        out_shape=jax.ShapeDtypeStruct((T, NSA_HEADS * HEAD_DIM), BF16),
        grid_spec=grid_spec,
        compiler_params=_params(("parallel", "parallel", "arbitrary")),
        name="nsa_attend",
    )(slopes, act_flat, P, k_bf16, vt_tiles, k_bf16, vt_tiles, selt, oct, P, ng)


def _kv_prep_kernel(k_ref, v_ref, ko_ref, vo_ref):
    ko_ref[0] = k_ref[0].astype(ko_ref.dtype)
    for c in range(vo_ref.shape[1]):
        vo_ref[0, c] = v_ref[0, c * KEY_TILE:(c + 1) * KEY_TILE, :].T.astype(vo_ref.dtype)


def _attend_operands(P, B, S, tm=1024):
    G = NSA_KV_GROUPS
    T = B * S
    tm = min(tm, S)
    per_step = tm // KEY_TILE

    def slab(first):
        return lambda s, i: (first + (s // G) * (SLAB_K_WIN - SLAB_K_SEL) + s % G, i, 0)

    return pl.pallas_call(
        _kv_prep_kernel,
        out_shape=[jax.ShapeDtypeStruct((2 * G, T, LANE), BF16),
                   jax.ShapeDtypeStruct((2 * G, T // KEY_TILE, HEAD_DIM, KEY_TILE), BF16)],
        grid=(2 * G, T // tm),
        in_specs=[pl.BlockSpec((1, tm, LANE), slab(SLAB_K_SEL)), pl.BlockSpec((1, tm, LANE), slab(SLAB_V_SEL))],
        out_specs=[pl.BlockSpec((1, tm, LANE), lambda s, i: (s, i, 0)),
                   pl.BlockSpec((1, per_step, HEAD_DIM, KEY_TILE), lambda s, i: (s, i, 0, 0))],
        compiler_params=_params(("parallel", "parallel")),
        name="kv_prep",
    )(P, P)


def _active_tiles(act, S):
    return act[:, :, :, 0, :S // KEY_TILE].reshape(-1)


def _out_proj_kernel(ydn_ref, ynsa_ref, x_ref, wo_ref, g_ref, wrh_ref, wrl_ref, br_ref,
                     x1_ref, h2_ref, ri_ref, rw_ref, cnt_ref, carry_scr):
    tm = x_ref.shape[0]
    half = ynsa_ref.shape[1]

    @pl.when(pl.program_id(0) == 0)
    def _():
        carry_scr[...] = jnp.zeros_like(carry_scr)

    ydn = jnp.concatenate([ydn_ref[hh] for hh in range(ydn_ref.shape[0])], axis=1)
    x1 = x_ref[...] + _dot(ydn, wo_ref[0:half, :]) + _dot(ynsa_ref[...], wo_ref[half:, :])
    x1_ref[...] = x1
    h2 = x1 * lax.rsqrt(jnp.mean(x1 * x1, axis=-1, keepdims=True) + NORM_EPS) * g_ref[...]
    h2_ref[...] = h2

    hi = h2.astype(BF16)
    lo = (h2 - hi.astype(F32)).astype(BF16)
    lg = _dot(hi, wrh_ref[...]) + _dot(hi, wrl_ref[...]) + _dot(lo, wrh_ref[...])
    lane = lax.broadcasted_iota(I32, (1, LANE), 1)
    lanef = lane.astype(F32)
    biased = lg + br_ref[...]

    is_grp = lane < N_GROUPS
    gmax = jnp.max(jnp.where(is_grp, lg, NEG_INF), axis=-1, keepdims=True)
    eg = jnp.where(is_grp, jnp.exp(lg - gmax), 0.0)
    pg = eg / jnp.sum(eg, axis=-1, keepdims=True)
    gb = jnp.where(is_grp, biased, NEG_INF)
    g_sel = jnp.min(jnp.where(gb == jnp.max(gb, axis=-1, keepdims=True), lanef, float(LANE)),
                    axis=-1, keepdims=True)
    p_sel = _pick_lane(pg, lanef, g_sel)

    lo_lane = N_GROUPS + g_sel * EXPERTS_PER_GROUP
    in_grp = (lanef >= lo_lane) & (lanef < lo_lane + EXPERTS_PER_GROUP)
    v1 = jnp.where(in_grp, biased, NEG_INF)
    i1 = jnp.min(jnp.where(v1 == jnp.max(v1, axis=-1, keepdims=True), lanef, float(LANE)),
                 axis=-1, keepdims=True)
    v2 = jnp.where(lanef == i1, NEG_INF * 2, v1)
    i2 = jnp.min(jnp.where(v2 == jnp.max(v2, axis=-1, keepdims=True), lanef, float(LANE)),
                 axis=-1, keepdims=True)
    el1 = _pick_lane(lg, lanef, i1)
    el2 = _pick_lane(lg, lanef, i2)
    mx = jnp.maximum(el1, el2)
    e1 = jnp.exp(el1 - mx)
    e2 = jnp.exp(el2 - mx)
    rw_ref[...] = jnp.where(lane == 0, p_sel * (e1 / (e1 + e2)),
                            jnp.where(lane == 1, p_sel * (e2 / (e1 + e2)), 0.0))

    ex1 = i1 - N_GROUPS
    ex2 = i2 - N_GROUPS
    oh1 = lanef == ex1
    oh2 = lanef == ex2
    both = jnp.where(oh1 | oh2, 1.0, 0.0)
    ri = lax.broadcasted_iota(I32, (tm, tm), 0)
    ci = lax.broadcasted_iota(I32, (tm, tm), 1)
    before = _dot((ci < ri).astype(BF16), both.astype(BF16)) + carry_scr[...]
    r1 = jnp.sum(jnp.where(oh1, before, 0.0), axis=-1, keepdims=True)
    r2 = jnp.sum(jnp.where(oh2, before, 0.0), axis=-1, keepdims=True)
    packed = jnp.where(lane == 0, ex1, jnp.where(lane == 1, ex2,
                       jnp.where(lane == 2, r1, jnp.where(lane == 3, r2, 0.0))))
    ri_ref[...] = packed.astype(I32)
    total = carry_scr[...] + jnp.sum(both, axis=0, keepdims=True)
    carry_scr[...] = total
    cnt_ref[...] = jnp.broadcast_to(total, cnt_ref.shape).astype(I32)


def _out_proj(ydn, ynsa, x2d, wo, g, wr_hi, wr_lo, br, tm=256):
    T, D = x2d.shape
    half = ynsa.shape[1]
    row = lambda i: (i, 0)
    fixed = lambda i: (0, 0)
    return pl.pallas_call(
        _out_proj_kernel,
        out_shape=[
            jax.ShapeDtypeStruct((T, D), F32), jax.ShapeDtypeStruct((T, D), F32),
            jax.ShapeDtypeStruct((T, LANE), I32), jax.ShapeDtypeStruct((T, LANE), F32),
            jax.ShapeDtypeStruct((8, LANE), I32),
        ],
        grid=(T // tm,),
        in_specs=[
            pl.BlockSpec((ydn.shape[0], tm, LANE), lambda i: (0, i, 0)),
            pl.BlockSpec((tm, half), row), pl.BlockSpec((tm, D), row),
            pl.BlockSpec((D, D), fixed), pl.BlockSpec((1, D), fixed),
            pl.BlockSpec((D, LANE), fixed), pl.BlockSpec((D, LANE), fixed), pl.BlockSpec((1, LANE), fixed),
        ],
        out_specs=[
            pl.BlockSpec((tm, D), row), pl.BlockSpec((tm, D), row),
            pl.BlockSpec((tm, LANE), row), pl.BlockSpec((tm, LANE), row), pl.BlockSpec((8, LANE), fixed),
        ],
        scratch_shapes=[pltpu.VMEM((1, LANE), F32)],
        compiler_params=_params(("arbitrary",)),
        name="out_proj_router",
    )(ydn, ynsa, x2d, wo, g, wr_hi, wr_lo, br)


def _row_copy(src_hbm, row, dst, slot, sem):
    return pltpu.make_async_copy(src_hbm.at[pl.ds(row, 1)], dst.at[pl.ds(slot, 1)], sem)


GATHER_UNROLL = 8


def _gather_rows(src_hbm, idx_ref, base, dst, sem, n_rows):
    def issue(r, c):
        _row_copy(src_hbm, idx_ref[base + r], dst, r, sem).start()
        return c
    lax.fori_loop(0, n_rows, issue, 0, unroll=GATHER_UNROLL)


def _wait_rows(src_hbm, dst, sem):
    pltpu.make_async_copy(src_hbm.at[pl.ds(0, dst.shape[0])], dst, sem).wait()


def _moe_experts_kernel(be_ref, nu_ref, rt_ref, h_hbm, wg_ref, wu_ref, wd_ref, y_ref,
                        xs_a, xs_b, wg_scr, wu_scr, wd_scr, sem):
    i = pl.program_id(0)
    R = xs_a.shape[0]
    n_used = nu_ref[0]

    @pl.when((i == 0) & (n_used > 0))
    def _():
        _gather_rows(h_hbm, rt_ref, 0, xs_a, sem.at[0], R)

    @pl.when((i < n_used) & ((i == 0) | (be_ref[jnp.maximum(i - 1, 0)] != be_ref[i])))
    def _():
        wg_scr[...] = wg_ref[0].astype(BF16)
        wu_scr[...] = wu_ref[0].astype(BF16)
        wd_scr[...] = wd_ref[0].astype(BF16)

    def block(cur, nxt, s, prefetch):
        _wait_rows(h_hbm, cur, sem.at[s])
        if prefetch:
            for r in range(R):
                _row_copy(h_hbm, rt_ref[(i + 1) * R + r], nxt, r, sem.at[1 - s]).start()
        xb = cur[...].astype(BF16)
        hid = _silu(_dot(xb, wg_scr[...])) * _dot(xb, wu_scr[...])
        y_ref[...] = _dot(hid.astype(BF16), wd_scr[...])

    for s, (cur, nxt) in enumerate(((xs_a, xs_b), (xs_b, xs_a))):
        @pl.when((i % 2 == s) & (i + 1 < n_used))
        def _():
            block(cur, nxt, s, True)

        @pl.when((i % 2 == s) & (i + 1 == n_used))
        def _():
            block(cur, nxt, s, False)

    @pl.when(i >= n_used)
    def _():
        y_ref[...] = jnp.zeros_like(y_ref)


def _moe_experts(h2, block_expert, n_used, row_tok, w_gate, w_up, w_down):
    T, D = h2.shape
    R = MOE_ROWS
    n_blocks = block_expert.shape[0]
    DE = w_gate.shape[2]
    grid_spec = pltpu.PrefetchScalarGridSpec(
        num_scalar_prefetch=3,
        grid=(n_blocks,),
        in_specs=[
            pl.BlockSpec(memory_space=pl.ANY),
            pl.BlockSpec((1, D, DE), lambda i, be, nu, rt: (be[i], 0, 0)),
            pl.BlockSpec((1, D, DE), lambda i, be, nu, rt: (be[i], 0, 0)),
            pl.BlockSpec((1, DE, D), lambda i, be, nu, rt: (be[i], 0, 0)),
        ],
        out_specs=pl.BlockSpec((R, D), lambda i, be, nu, rt: (i, 0)),
        scratch_shapes=[
            pltpu.VMEM((R, D), F32), pltpu.VMEM((R, D), F32), pltpu.VMEM((D, DE), BF16),
            pltpu.VMEM((D, DE), BF16), pltpu.VMEM((DE, D), BF16), pltpu.SemaphoreType.DMA((2,)),
        ],
    )
    return pl.pallas_call(
        _moe_experts_kernel,
        out_shape=jax.ShapeDtypeStruct((n_blocks * R, D), F32),
        grid_spec=grid_spec,
        compiler_params=_params(("arbitrary",)),
        name="moe_experts",
    )(block_expert, n_used, row_tok, h2, w_gate, w_up, w_down)


def _moe_combine_kernel(dest_ref, x1_ref, rw_ref, g_ref, ys_hbm, o_ref, buf_a, buf_b, sem):
    i = pl.program_id(0)
    n_steps = pl.num_programs(0)
    tm = x1_ref.shape[0]
    T = tm * n_steps

    @pl.when(i == 0)
    def _():
        for k in range(2):
            _gather_rows(ys_hbm, dest_ref, k * T, buf_a.at[k], sem.at[0], tm)

    def step(cur, nxt, s, prefetch):
        for k in range(2):
            _wait_rows(ys_hbm, cur.at[k], sem.at[s])
        if prefetch:
            for k in range(2):
                for r in range(tm):
                    _row_copy(ys_hbm, dest_ref[k * T + (i + 1) * tm + r], nxt.at[k], r, sem.at[1 - s]).start()
        rw = rw_ref[...]
        x2 = x1_ref[...] + cur[0] * rw[:, 0:1] + cur[1] * rw[:, 1:2]
        o_ref[...] = x2 * lax.rsqrt(jnp.mean(x2 * x2, axis=-1, keepdims=True) + NORM_EPS) * g_ref[...]

    for s, (cur, nxt) in enumerate(((buf_a, buf_b), (buf_b, buf_a))):
        @pl.when((i % 2 == s) & (i + 1 < n_steps))
        def _():
            step(cur, nxt, s, True)

        @pl.when((i % 2 == s) & (i + 1 == n_steps))
        def _():
            step(cur, nxt, s, False)


def _moe_combine(dest_flat, x1, rw, g, ys, tm=128):
    T, D = x1.shape
    grid_spec = pltpu.PrefetchScalarGridSpec(
        num_scalar_prefetch=1,
        grid=(T // tm,),
        in_specs=[
            pl.BlockSpec((tm, D), lambda i, d: (i, 0)),
            pl.BlockSpec((tm, LANE), lambda i, d: (i, 0)),
            pl.BlockSpec((1, D), lambda i, d: (0, 0)),
            pl.BlockSpec(memory_space=pl.ANY),
        ],
        out_specs=pl.BlockSpec((tm, D), lambda i, d: (i, 0)),
        scratch_shapes=[pltpu.VMEM((2, tm, D), F32), pltpu.VMEM((2, tm, D), F32),
                        pltpu.SemaphoreType.DMA((2,))],
    )
    return pl.pallas_call(
        _moe_combine_kernel,
        out_shape=jax.ShapeDtypeStruct((T, D), F32),
        grid_spec=grid_spec,
        compiler_params=_params(("arbitrary",)),
        name="moe_combine",
    )(dest_flat, x1, rw, g, ys)


def _slab_weights(w_in):
    D = w_in.shape[0]
    n_dn = 4 * DN_HEADS * HEAD_DIM
    n_small_a = 2 * DN_HEADS
    n_nsa = NSA_HEADS * HEAD_DIM + 6 * NSA_KV_GROUPS * HEAD_DIM
    c0 = n_dn + n_small_a
    c1 = c0 + n_nsa
    n_gate = 3 * NSA_HEADS
    pad = N_SLABS * LANE - (n_dn + n_nsa + n_small_a + n_gate)
    return jnp.concatenate(
        [w_in[:, :n_dn], w_in[:, c0:c1], w_in[:, n_dn:c0], w_in[:, c1:c1 + n_gate],
         jnp.zeros((D, pad), w_in.dtype)], axis=1).astype(BF16)


def _layer(x, norm_mix_g, w_in, conv_w, dt_bias, a_log, dn_norm_g, cmp_pos_k, cmp_w1_k, cmp_w2_k,
           cmp_pos_v, cmp_w1_v, cmp_w2_v, nsa_norm_g, w_out, norm_ffn_g, w_group, b_group,
           w_router, b_router, w_gate, w_up, w_down, norm_final_g):
    B, S, D = x.shape
    T = B * S
    x2d = x.reshape(T, D)

    P = _in_proj(x2d, norm_mix_g.reshape(1, D), _slab_weights(w_in))
    ydn = _deltanet(P, B, S, conv_w, dt_bias, a_log, dn_norm_g)

    kvc, kvct = _compress(P, B, S, jnp.stack([cmp_pos_k, cmp_pos_v]), jnp.stack([cmp_w1_k, cmp_w1_v]),
                          jnp.stack([cmp_w2_k, cmp_w2_v]))
    oct, selt, act = _nsa_select(P, kvc, kvct, B, S)
    ynsa = _nsa_attend(P, *_attend_operands(P, B, S), selt, _active_tiles(act, S), oct, B, S, nsa_norm_g)

    wr = jnp.zeros((D, LANE), F32).at[:, :N_GROUPS].set(w_group).at[:, N_GROUPS:N_GROUPS + N_EXPERTS].set(w_router)
    wr_hi = wr.astype(BF16)
    wr_lo = (wr - wr_hi.astype(F32)).astype(BF16)
    br = jnp.zeros((1, LANE), F32).at[0, :N_GROUPS].set(b_group).at[0, N_GROUPS:N_GROUPS + N_EXPERTS].set(b_router)
    x1, h2, ri, rw, cnt = _out_proj(ydn, ynsa, x2d, w_out.astype(BF16), norm_ffn_g.reshape(1, D),
                                    wr_hi, wr_lo, br)

    R = MOE_ROWS
    counts = cnt[0, :N_EXPERTS]
    padded = (counts + R - 1) // R * R
    pends = jnp.cumsum(padded)
    pstarts = pends - padded
    first_row = jnp.sum(jnp.where(ri[:, 0:2, None] == jnp.arange(N_EXPERTS, dtype=I32), pstarts, 0), axis=-1)
    dest = first_row + ri[:, 2:4]
    dest_flat = dest.T.reshape(-1).astype(I32)
    n_blocks = (2 * T + R - 1) // R + N_EXPERTS
    block_start = jnp.arange(n_blocks, dtype=I32) * R
    block_expert = jnp.minimum(jnp.sum(block_start[:, None] >= pends[None, :], axis=1),
                               N_EXPERTS - 1).astype(I32)
    n_used = (pends[-1:] // R).astype(I32)
    row_tok = jnp.zeros((n_blocks * R,), I32).at[dest_flat].set(jnp.tile(jnp.arange(T, dtype=I32), 2))

    ys = _moe_experts(h2, block_expert, n_used, row_tok, w_gate, w_up, w_down)
    out = _moe_combine(dest_flat, x1, rw, norm_final_g.reshape(1, D), ys)
    return out.reshape(B, S, D)


def kernel(x, norm_mix_g, w_in, conv_w, dt_bias, a_log, dn_norm_g, cmp_pos_k, cmp_w1_k, cmp_w2_k,
           cmp_pos_v, cmp_w1_v, cmp_w2_v, nsa_norm_g, w_out, norm_ffn_g, w_group, b_group,
           w_router, b_router, w_gate, w_up, w_down, norm_final_g):
    assert w_in.shape[0] == 1, "one layer"
    return _layer(x, norm_mix_g[0], w_in[0], conv_w[0], dt_bias[0], a_log[0], dn_norm_g[0],
                  cmp_pos_k[0], cmp_w1_k[0], cmp_w2_k[0], cmp_pos_v[0], cmp_w1_v[0], cmp_w2_v[0],
                  nsa_norm_g[0], w_out[0], norm_ffn_g[0], w_group[0], b_group[0], w_router[0],
                  b_router[0], w_gate[0], w_up[0], w_down[0], norm_final_g)
```

```python
import functools
import math

import numpy as np
import jax
import jax.numpy as jnp
from jax import lax
from jax.experimental import pallas as pl
from jax.experimental.pallas import tpu as pltpu

F32 = jnp.float32
BF16 = jnp.bfloat16
I32 = jnp.int32

LANE = 128
HEAD_DIM = 128
DN_HEADS = 8
NSA_HEADS = 8
NSA_KV_GROUPS = 2
NSA_HPG = NSA_HEADS // NSA_KV_GROUPS
CONV_WIDTH = 4
DN_CHUNK = 64
CMP_LEN = 32
CMP_STRIDE = 16
SEL_LEN = 64
SEL_COUNT = 16
WINDOW = 512
Q_BLOCK = 128
N_GROUPS = 8
EXPERTS_PER_GROUP = 8
N_EXPERTS = N_GROUPS * EXPERTS_PER_GROUP
NORM_EPS = 1e-6
NEG_INF = -1e30
SEL_FORCE = 1e9
PICKED = -3e38

SLAB_DN_Q, SLAB_DN_K, SLAB_DN_V, SLAB_DN_Z = 0, 8, 16, 24
SLAB_NSA_Q = 32
SLAB_K_CMP, SLAB_V_CMP, SLAB_K_SEL, SLAB_V_SEL, SLAB_K_WIN, SLAB_V_WIN = 40, 42, 44, 46, 48, 50
SLAB_SMALL = 52
N_SLABS = 54
LANE_DN_B, LANE_DN_A, LANE_GATE = 0, 8, 16

KEY_TILE = 256
MOE_ROWS = 256

VMEM_LIMIT = 56 * 1024 * 1024


def _params(sem, vmem=VMEM_LIMIT):
    return pltpu.CompilerParams(dimension_semantics=sem, vmem_limit_bytes=vmem)


def _dot(a, b):
    return jnp.dot(a, b, preferred_element_type=F32)


def _dot_nt(a, b):
    return lax.dot_general(a, b, (((1,), (1,)), ((), ())), preferred_element_type=F32)


def _dot_ones(ones_mask, x):
    n = x.shape[1]
    hi = x.astype(BF16)
    rest = x - hi.astype(F32)
    mid = rest.astype(BF16)
    lo = (rest - mid.astype(F32)).astype(BF16)
    r = _dot(ones_mask.astype(BF16), jnp.concatenate([hi, mid, lo], axis=1))
    return r[:, 0:n] + r[:, n:2 * n] + r[:, 2 * n:3 * n]


def _sigmoid(x):
    return 1.0 / (1.0 + jnp.exp(-x))


def _silu(x):
    return x * _sigmoid(x)


def _pick_lane(x, lane_ids, idx):
    return jnp.sum(jnp.where(lane_ids == idx, x, 0.0), axis=1, keepdims=True)


def _in_proj_kernel(x_ref, g_ref, w_ref, o_ref, h_scr):
    @pl.when(pl.program_id(1) == 0)
    def _():
        xf = x_ref[...]
        ms = jnp.mean(xf * xf, axis=-1, keepdims=True)
        h_scr[...] = (xf * lax.rsqrt(ms + NORM_EPS) * g_ref[...]).astype(BF16)

    acc = _dot(h_scr[...], w_ref[...])
    for k in range(o_ref.shape[0]):
        o_ref[k] = acc[:, k * LANE:(k + 1) * LANE]


def _in_proj(x2d, g, w_slabs, tm=1024, tn=768):
    T, D = x2d.shape
    NP = w_slabs.shape[1]
    ns = tn // LANE
    return pl.pallas_call(
        _in_proj_kernel,
        out_shape=jax.ShapeDtypeStruct((NP // LANE, T, LANE), F32),
        grid=(T // tm, NP // tn),
        in_specs=[
            pl.BlockSpec((tm, D), lambda i, j: (i, 0)),
            pl.BlockSpec((1, D), lambda i, j: (0, 0)),
            pl.BlockSpec((D, tn), lambda i, j: (0, j)),
        ],
        out_specs=pl.BlockSpec((ns, tm, LANE), lambda i, j: (j, i, 0)),
        scratch_shapes=[pltpu.VMEM((tm, D), BF16)],
        compiler_params=_params(("parallel", "arbitrary")),
        name="in_proj",
    )(x2d, g, w_slabs)


DN_HEADS_PER_ITER = 8


def _deltanet_kernel(p_ref, sm_ref, cw_ref, alog_ref, dtb_ref, ng_ref, o_ref,
                     ext_scr, s_scr, gc_scr, gl_scr, gct_scr, beta_scr):
    L = p_ref.shape[1]
    C = DN_CHUNK
    H = DN_HEADS

    @pl.when(pl.program_id(1) == 0)
    def _():
        ext_scr[:, 0:8, :] = jnp.zeros((3 * H, 8, LANE), F32)
        s_scr[...] = jnp.zeros_like(s_scr)

    sm = sm_ref[0]
    lane = lax.broadcasted_iota(I32, (1, LANE), 1)
    xa = sm + dtb_ref[...]
    softplus = jnp.maximum(xa, 0.0) + jnp.log1p(jnp.exp(-jnp.abs(xa)))
    ld = -jnp.exp(alog_ref[...]) * softplus
    ri = lax.broadcasted_iota(I32, (L, L), 0)
    ci = lax.broadcasted_iota(I32, (L, L), 1)
    same = (ri // C) == (ci // C)
    causal = same & (ci <= ri)
    gc_slab = _dot_ones(jnp.where(causal, 1.0, 0.0), ld)
    gc_scr[...] = gc_slab
    gct_scr[...] = gc_slab.T
    gl_scr[...] = _dot_ones(jnp.where(same, 1.0, 0.0), ld)
    beta_scr[...] = _sigmoid(sm)

    def conv_silu(slab, h):
        idx = slab + h
        ext_scr[idx, 8:, :] = p_ref[idx]
        w = cw_ref[idx]
        acc = ext_scr[idx, pl.ds(8 - (CONV_WIDTH - 1), L), :] * w[0:1]
        for j in range(1, CONV_WIDTH):
            acc = acc + ext_scr[idx, pl.ds(8 - (CONV_WIDTH - 1) + j, L), :] * w[j:j + 1]
        ext_scr[idx, 0:8, :] = ext_scr[idx, L:L + 8, :]
        return _silu(acc)

    def l2n(x):
        return x * lax.rsqrt(jnp.sum(x * x, axis=-1, keepdims=True) + NORM_EPS)

    def head_group(hg, carry):
        hs = [hg * DN_HEADS_PER_ITER + dh for dh in range(DN_HEADS_PER_ITER)]
        n = range(len(hs))
        qs = [l2n(conv_silu(SLAB_DN_Q, h)) * (HEAD_DIM ** -0.5) for h in hs]
        ks = [l2n(conv_silu(SLAB_DN_K, h)) for h in hs]
        vs = [conv_silu(SLAB_DN_V, h) for h in hs]
        betas = [_pick_lane(beta_scr[...], lane, LANE_DN_B + h) for h in hs]
        gcols = [_pick_lane(gc_scr[...], lane, LANE_DN_A + h) for h in hs]
        glasts = [_pick_lane(gl_scr[...], lane, LANE_DN_A + h) for h in hs]
        grows = [gct_scr[pl.ds(LANE_DN_A + h, 1), :] for h in hs]
        states = [s_scr[h] for h in hs]
        zs = [p_ref[SLAB_DN_Z + h] for h in hs]

        decays = [jnp.where(causal, jnp.exp(jnp.where(causal, gcols[i] - grows[i], 0.0)), 0.0) for i in n]
        egcs = [jnp.exp(gcols[i]) for i in n]
        kbs = [ks[i] * betas[i] for i in n]
        kts = [ks[i].astype(BF16) for i in n]
        lms = [jnp.where(ci < ri, _dot_nt(kbs[i].astype(BF16), kts[i]) * decays[i], 0.0) for i in n]
        row_c = lax.broadcasted_iota(I32, (C, L), 0)
        lane_c = lax.broadcasted_iota(I32, (C, L), 1)

        def side_by_side(m):
            out = jnp.where(lane_c // C == 0, m[0:C], 0.0)
            for c in range(1, L // C):
                out = out + jnp.where(lane_c // C == c, m[c * C:(c + 1) * C], 0.0)
            return out

        def block_diag(m):
            return jnp.where(same, jnp.concatenate([m] * (L // C), axis=0), 0.0).astype(BF16)

        pws = [side_by_side(lms[i]) for i in n]
        tinvs = [jnp.where(row_c == lane_c % C, 1.0, 0.0) - pws[i] for i in n]
        for _ in range(int(math.log2(C)) - 1):
            pws = [_dot(pws[i].astype(BF16), block_diag(pws[i])) for i in n]
            tinvs = [tinvs[i] + _dot(tinvs[i].astype(BF16), block_diag(pws[i])) for i in n]
        tbs = [block_diag(tinvs[i]) for i in n]
        us = [_dot(tbs[i], (vs[i] * betas[i]).astype(BF16)) for i in n]
        ws = [_dot(tbs[i], (kbs[i] * egcs[i]).astype(BF16)) for i in n]
        qks = [_dot_nt(qs[i].astype(BF16), kts[i]) * decays[i] for i in n]
        q_decs = [(qs[i] * egcs[i]).astype(BF16) for i in n]
        k_dec_ts = [(ks[i] * jnp.exp(glasts[i] - gcols[i])).T for i in n]

        outs = [[] for _ in n]
        for c in range(L // C):
            sl = slice(c * C, (c + 1) * C)
            sbs = [states[i].astype(BF16) for i in n]
            vnbs = [(us[i][sl] - _dot(ws[i][sl].astype(BF16), sbs[i])).astype(BF16) for i in n]
            for i in n:
                outs[i].append(_dot(q_decs[i][sl], sbs[i]) + _dot(qks[i][sl, sl].astype(BF16), vnbs[i]))
            states = [states[i] * jnp.exp(glasts[i][c * C:c * C + 1, :])
                      + _dot(k_dec_ts[i][:, sl].astype(BF16), vnbs[i]) for i in n]

        for i, h in enumerate(hs):
            s_scr[h] = states[i]
            o = jnp.concatenate(outs[i], axis=0)
            on = o * lax.rsqrt(jnp.mean(o * o, axis=-1, keepdims=True) + NORM_EPS) * ng_ref[...]
            o_ref[h] = (on * _silu(zs[i])).astype(o_ref.dtype)
        return carry

    lax.fori_loop(0, H // DN_HEADS_PER_ITER, head_group, 0)


def _deltanet(P, B, S, conv_w, dt_bias, a_log, norm_g, L=256):
    T = B * S
    nL = S // L
    H = DN_HEADS
    cw = conv_w.reshape(CONV_WIDTH, 3 * H, LANE).transpose(1, 0, 2)
    alog = jnp.zeros((1, LANE), F32).at[0, LANE_DN_A:LANE_DN_A + H].set(a_log)
    dtb = jnp.zeros((1, LANE), F32).at[0, LANE_DN_A:LANE_DN_A + H].set(dt_bias)
    vec = pl.BlockSpec((1, LANE), lambda b, i: (0, 0))
    return pl.pallas_call(
        _deltanet_kernel,
        out_shape=jax.ShapeDtypeStruct((H, T, LANE), BF16),
        grid=(B, nL),
        in_specs=[
            pl.BlockSpec((4 * H, L, LANE), lambda b, i: (0, b * nL + i, 0)),
            pl.BlockSpec((1, L, LANE), lambda b, i: (SLAB_SMALL, b * nL + i, 0)),
            pl.BlockSpec((3 * H, CONV_WIDTH, LANE), lambda b, i: (0, 0, 0)),
            vec, vec, vec,
        ],
        out_specs=pl.BlockSpec((H, L, LANE), lambda b, i: (0, b * nL + i, 0)),
        scratch_shapes=[
            pltpu.VMEM((3 * H, L + 8, LANE), F32), pltpu.VMEM((H, HEAD_DIM, HEAD_DIM), F32),
            pltpu.VMEM((L, LANE), F32), pltpu.VMEM((L, LANE), F32), pltpu.VMEM((LANE, L), F32),
            pltpu.VMEM((L, LANE), F32),
        ],
        compiler_params=_params(("parallel", "arbitrary")),
        name="deltanet",
    )(P, P, cw, alog, dtb, norm_g.reshape(1, LANE))


def _gelu_tanh(x):
    return 0.5 * x * (1.0 + jnp.tanh(math.sqrt(2.0 / math.pi) * (x + 0.044715 * (x * x * x))))


def _compress_kernel(x_ref, pos_ref, w1_ref, w2_ref, o_ref, ot_ref):
    n = o_ref.shape[3]
    half = CMP_LEN // 2
    y1 = jnp.zeros((n, HEAD_DIM), F32)
    y2 = jnp.zeros((n, HEAD_DIM), F32)
    for l in range(half):
        xl = x_ref[0, pl.ds(l, n, stride=CMP_STRIDE), :]
        y1 = y1 + _dot((xl + pos_ref[0, l:l + 1, :]).astype(BF16), w1_ref[0, l].astype(BF16))
        y2 = y2 + _dot((xl + pos_ref[0, half + l:half + l + 1, :]).astype(BF16),
                       w1_ref[0, half + l].astype(BF16))
    hid = _gelu_tanh(y1 + pltpu.roll(y2, n - 1, 0))
    row = lax.broadcasted_iota(I32, (n, 1), 0)
    hb = jnp.where(row < n - 1, hid, 0.0).astype(BF16)
    w2 = w2_ref[0]
    o_ref[0, 0, 0] = _dot(hb, w2.astype(BF16)).astype(o_ref.dtype)
    ot_ref[0, 0, 0] = _dot_nt(w2.T.astype(BF16), hb).astype(ot_ref.dtype)


def _compress(P, B, S, pos, w1, w2):
    n = S // CMP_STRIDE
    G = NSA_KV_GROUPS
    return pl.pallas_call(
        _compress_kernel,
        out_shape=[jax.ShapeDtypeStruct((2, B, G, n, HEAD_DIM), BF16),
                   jax.ShapeDtypeStruct((2, B, G, HEAD_DIM, n), BF16)],
        grid=(2, B, G),
        in_specs=[
            pl.BlockSpec((1, S, LANE), lambda c, b, g: (SLAB_K_CMP + 2 * c + g, b, 0)),
            pl.BlockSpec((1, CMP_LEN, HEAD_DIM), lambda c, b, g: (c, 0, 0)),
            pl.BlockSpec((1, CMP_LEN, HEAD_DIM, HEAD_DIM), lambda c, b, g: (c, 0, 0, 0)),
            pl.BlockSpec((1, HEAD_DIM, HEAD_DIM), lambda c, b, g: (c, 0, 0)),
        ],
        out_specs=[pl.BlockSpec((1, 1, 1, n, HEAD_DIM), lambda c, b, g: (c, b, g, 0, 0)),
                   pl.BlockSpec((1, 1, 1, HEAD_DIM, n), lambda c, b, g: (c, b, g, 0, 0))],
        compiler_params=_params(("parallel", "parallel", "parallel")),
        name="compress",
    )(P, pos, w1, w2)


LOG2E = math.log2(math.e)
M_INIT = -1e29


def _q_transposed(q_ref):
    c1 = (HEAD_DIM ** -0.5) * LOG2E
    return jnp.concatenate([(q_ref[hh] * c1).T for hh in range(q_ref.shape[0])], axis=1).astype(BF16)


def _nsa_select_kernel(slopes_ref, q_ref, kc_ref, vct_ref, aggt_ref, grp_ref, oct_ref, selt_ref, act_ref):
    g = pl.program_id(1)
    blk = pl.program_id(2)
    Q, H = Q_BLOCK, NSA_HPG
    n = kc_ref.shape[3]
    ns = selt_ref.shape[3]
    st = _dot(kc_ref[0, 0, 0], _q_transposed(q_ref))
    vct = vct_ref[0, 0, 0]

    t = blk * Q + lax.broadcasted_iota(I32, (1, Q), 1)
    cpos = lax.broadcasted_iota(I32, (n, Q), 0) * CMP_STRIDE + (CMP_LEN - 1)
    valid = cpos <= t
    rel = (cpos - (blk * Q + Q - 1)).astype(F32)

    psum = jnp.zeros((n, Q), F32)
    for hh in range(H):
        sh = jnp.where(valid, st[:, hh * Q:(hh + 1) * Q] + rel * (slopes_ref[g * H + hh] * LOG2E), NEG_INF)
        e = jnp.where(valid, jnp.exp2(sh - jnp.max(sh, axis=0, keepdims=True)), 0.0)
        p = e * (1.0 / jnp.maximum(jnp.sum(e, axis=0, keepdims=True), 1e-30))
        oct_ref[0, 0, 0, hh * HEAD_DIM:(hh + 1) * HEAD_DIM, :] = _dot(vct, p.astype(BF16))
        psum = psum + p

    hi = psum.astype(BF16)
    lo = (psum - hi.astype(F32)).astype(BF16)
    imp = _dot(aggt_ref[...], hi) + _dot(aggt_ref[...], lo)

    sid = lax.broadcasted_iota(I32, (ns, Q), 0)
    sidf = sid.astype(F32)
    cur = t // SEL_LEN
    forced = (sid == 0) | (sid == cur) | (sid == cur - 1)
    valid_s = sid * SEL_LEN <= t
    work = jnp.where(forced, SEL_FORCE, jnp.where(valid_s, imp, NEG_INF))
    picked = jnp.zeros((ns, Q), F32)
    for _ in range(min(SEL_COUNT, ns)):
        m_w = jnp.max(work, axis=0, keepdims=True)
        first = jnp.min(jnp.where(work == m_w, sidf, float(ns)), axis=0, keepdims=True)
        hit = sidf == first
        picked = jnp.where(hit, 1.0, picked)
        work = jnp.where(hit, PICKED, work)
    selt = jnp.where(valid_s, picked, 0.0)
    selt_ref[0, 0, 0] = selt
    per_block = _dot_nt(jnp.ones((8, Q), BF16), selt.astype(BF16))
    tiles = _dot(per_block.astype(BF16), grp_ref[...])
    act_ref[0, 0, 0] = (tiles > 0.5).astype(I32)


def _cmp_to_sel_matrix(n_cmp_padded, n_sel):
    r = SEL_LEN // CMP_STRIDE
    c = CMP_LEN // CMP_STRIDE
    off = np.arange(n_cmp_padded)[:, None] - r * np.arange(n_sel)[None, :] + (c - 1)
    cnt = np.minimum(c - 1, off) - np.maximum(0, off - r + 1) + 1
    return np.clip(cnt, 0, None).astype(np.float32)


def _alibi_slopes(n_heads):
    return (2.0 ** (-8.0 * (np.arange(n_heads) + 1) / n_heads)).astype(np.float32)


def _nsa_select(P, kvc, kvct, B, S):
    G, Q, H = NSA_KV_GROUPS, Q_BLOCK, NSA_HPG
    nq = S // Q
    n = S // CMP_STRIDE
    ns = S // SEL_LEN
    aggt = jnp.asarray(_cmp_to_sel_matrix(n, ns).T, BF16)
    per_tile = KEY_TILE // SEL_LEN
    grp = jnp.asarray((np.arange(ns)[:, None] // per_tile) == np.arange(LANE)[None, :], BF16)
    slopes = jnp.asarray(_alibi_slopes(NSA_HEADS))
    per_blk = lambda b, g, i, s_: (b, g, i, 0, 0)
    grid_spec = pltpu.PrefetchScalarGridSpec(
        num_scalar_prefetch=1,
        grid=(B, G, nq),
        in_specs=[
            pl.BlockSpec((H, Q, LANE), lambda b, g, i, s_: (SLAB_NSA_Q // H + g, b * nq + i, 0)),
            pl.BlockSpec((1, 1, 1, n, HEAD_DIM), lambda b, g, i, s_: (0, b, g, 0, 0)),
            pl.BlockSpec((1, 1, 1, HEAD_DIM, n), lambda b, g, i, s_: (1, b, g, 0, 0)),
            pl.BlockSpec((ns, n), lambda b, g, i, s_: (0, 0)),
            pl.BlockSpec((ns, LANE), lambda b, g, i, s_: (0, 0)),
        ],
        out_specs=[
            pl.BlockSpec((1, 1, 1, H * HEAD_DIM, Q), per_blk),
            pl.BlockSpec((1, 1, 1, ns, Q), per_blk),
            pl.BlockSpec((1, 1, 1, 8, LANE), per_blk),
        ],
    )
    return pl.pallas_call(
        _nsa_select_kernel,
        out_shape=[
            jax.ShapeDtypeStruct((B, G, nq, H * HEAD_DIM, Q), F32),
            jax.ShapeDtypeStruct((B, G, nq, ns, Q), F32),
            jax.ShapeDtypeStruct((B, G, nq, 8, LANE), I32),
        ],
        grid_spec=grid_spec,
        compiler_params=_params(("parallel", "parallel", "parallel")),
        name="nsa_select",
    )(slopes, P, kvc, kvct, aggt, grp)


def _nsa_attend_kernel(slopes_ref, act_ref, q_ref, ks_ref, vst_ref, kw_ref, vwt_ref, selt_ref, oct_ref,
                       sm_ref, ng_ref, o_ref, m_scr, l_scr, acc_scr, sig_scr, *, tiles_max):
    b = pl.program_id(0)
    g = pl.program_id(1)
    blk = pl.program_id(2)
    nq = pl.num_programs(2)
    Q, H = Q_BLOCK, NSA_HPG
    per_tile = KEY_TILE // SEL_LEN

    def per_head(x):
        return jnp.concatenate([x] * H, axis=1)

    qt = _q_transposed(q_ref)
    t = blk * Q + lax.broadcasted_iota(I32, (1, Q), 1)
    t_last = blk * Q + Q - 1
    slope_row = jnp.concatenate(
        [jnp.full((1, Q), slopes_ref[g * H + hh] * LOG2E, F32) for hh in range(H)], axis=1)
    key_in_tile = lax.broadcasted_iota(I32, (KEY_TILE, Q), 0)
    diag = blk // (KEY_TILE // Q)

    def reset():
        m_scr[...] = jnp.full_like(m_scr, M_INIT)
        l_scr[...] = jnp.zeros_like(l_scr)
        acc_scr[...] = jnp.zeros_like(acc_scr)

    def update(k_ref, vt_ref, tiles):
        scores = []
        for j, mask_fn in tiles:
            start = pl.multiple_of(j * KEY_TILE, KEY_TILE)
            pos = start + key_in_tile
            s = _dot(k_ref[0, pl.ds(start, KEY_TILE), :], qt) + per_head((pos - t_last).astype(F32)) * slope_row
            scores.append(jnp.where(per_head(mask_fn(pos)), s, NEG_INF))
        m_old = m_scr[...]
        m_new = m_old
        for s in scores:
            m_new = jnp.maximum(m_new, jnp.max(s, axis=0, keepdims=True))
        alpha = jnp.exp2(m_old - m_new)
        l = alpha * l_scr[...]
        acc = alpha * acc_scr[...]
        for (j, _), s in zip(tiles, scores):
            p = jnp.exp2(s - m_new)
            l = l + jnp.sum(p, axis=0, keepdims=True)
            acc = acc + _dot(vt_ref[0, j], p.astype(BF16))
        l_scr[...] = l
        acc_scr[...] = acc
        m_scr[...] = m_new

    def finish():
        return acc_scr[...] * (1.0 / jnp.maximum(l_scr[...], 1e-30))

    def picked(rows):
        spread = jnp.concatenate(
            [jnp.broadcast_to(rows[c:c + 1], (SEL_LEN, Q)) for c in range(per_tile)], axis=0)
        return lambda pos: jnp.where(pos <= t, spread, 0.0) > 0.5

    reset()
    act_base = ((b * pl.num_programs(1) + g) * nq + blk) * tiles_max

    def pair_body(jp, carry):
        @pl.when((act_ref[act_base + 2 * jp] > 0) | (act_ref[act_base + 2 * jp + 1] > 0))
        def _():
            rows = selt_ref[0, 0, 0, pl.ds(pl.multiple_of(jp * 2 * per_tile, 2 * per_tile), 2 * per_tile), :]
            update(ks_ref, vst_ref, [(2 * jp, picked(rows[0:per_tile])),
                                     (2 * jp + 1, picked(rows[per_tile:2 * per_tile]))])
        return carry

    lax.fori_loop(0, diag // 2 + 1, pair_body, 0)
    o_sel = finish()

    def in_window(exists):
        def mask(pos):
            dist = jnp.where(pos <= t, t - pos, WINDOW)
            return jnp.where(exists, dist, WINDOW) < WINDOW
        return mask

    reset()
    update(kw_ref, vwt_ref, [(jnp.maximum(diag - back, 0), in_window(diag - back >= 0))
                             for back in range(WINDOW // KEY_TILE, -1, -1)])
    o_win = finish()

    sig_scr[...] = _sigmoid(sm_ref[0].T)
    for hh in range(H):
        cols = slice(hh * Q, (hh + 1) * Q)
        gl = LANE_GATE + (g * H + hh) * 3
        o = (sig_scr[pl.ds(gl, 1), :] * oct_ref[0, 0, 0, hh * HEAD_DIM:(hh + 1) * HEAD_DIM, :]
             + sig_scr[pl.ds(gl + 1, 1), :] * o_sel[:, cols] + sig_scr[pl.ds(gl + 2, 1), :] * o_win[:, cols])
        on = o * lax.rsqrt(jnp.mean(o * o, axis=0, keepdims=True) + NORM_EPS) * ng_ref[...]
        o_ref[:, hh * HEAD_DIM:(hh + 1) * HEAD_DIM] = on.T.astype(o_ref.dtype)


def _nsa_attend(P, k_bf16, vt_tiles, selt, act_flat, oct, B, S, norm_g):
    G, Q, H = NSA_KV_GROUPS, Q_BLOCK, NSA_HPG
    nq = S // Q
    ns = S // SEL_LEN
    nt = S // KEY_TILE
    T = B * S
    slopes = jnp.asarray(_alibi_slopes(NSA_HEADS))
    ng = jnp.broadcast_to(norm_g[:, None], (HEAD_DIM, Q))
    per_blk = lambda b, g, i, *_: (b, g, i, 0, 0)

    def k_spec(base):
        return pl.BlockSpec((1, S, LANE), lambda b, g, i, *_: (base + g, b, 0))

    def vt_spec(base):
        return pl.BlockSpec((1, nt, HEAD_DIM, KEY_TILE), lambda b, g, i, *_: (base + g, b, 0, 0))

    grid_spec = pltpu.PrefetchScalarGridSpec(
        num_scalar_prefetch=2,
        grid=(B, G, nq),
        in_specs=[
            pl.BlockSpec((H, Q, LANE), lambda b, g, i, *_: (SLAB_NSA_Q // H + g, b * nq + i, 0)),
            k_spec(0), vt_spec(0), k_spec(G), vt_spec(G),
            pl.BlockSpec((1, 1, 1, ns, Q), per_blk),
            pl.BlockSpec((1, 1, 1, H * HEAD_DIM, Q), per_blk),
            pl.BlockSpec((1, Q, LANE), lambda b, g, i, *_: (SLAB_SMALL, b * nq + i, 0)),
            pl.BlockSpec((HEAD_DIM, Q), lambda b, g, i, *_: (0, 0)),
        ],
        out_specs=pl.BlockSpec((Q, H * HEAD_DIM), lambda b, g, i, *_: (b * nq + i, g)),
        scratch_shapes=[
            pltpu.VMEM((1, H * Q), F32), pltpu.VMEM((1, H * Q), F32),
            pltpu.VMEM((HEAD_DIM, H * Q), F32), pltpu.VMEM((LANE, Q), F32),
        ],
    )
    return pl.pallas_call(
        functools.partial(_nsa_attend_kernel, tiles_max=nt),
        out_shape=jax.ShapeDtypeStruct((T, NSA_HEADS * HEAD_DIM), BF16),
        grid_spec=grid_spec,
        compiler_params=_params(("parallel", "parallel", "arbitrary")),
        name="nsa_attend",
    )(slopes, act_flat, P, k_bf16, vt_tiles, k_bf16, vt_tiles, selt, oct, P, ng)


def _kv_prep_kernel(k_ref, v_ref, ko_ref, vo_ref):
    ko_ref[0] = k_ref[0].astype(ko_ref.dtype)
    for c in range(vo_ref.shape[1]):
        vo_ref[0, c] = v_ref[0, c * KEY_TILE:(c + 1) * KEY_TILE, :].T.astype(vo_ref.dtype)


def _attend_operands(P, B, S, tm=1024):
    G = NSA_KV_GROUPS
    T = B * S
    tm = min(tm, S)
    per_step = tm // KEY_TILE

    def slab(first):
        return lambda s, i: (first + (s // G) * (SLAB_K_WIN - SLAB_K_SEL) + s % G, i, 0)

    return pl.pallas_call(
        _kv_prep_kernel,
        out_shape=[jax.ShapeDtypeStruct((2 * G, T, LANE), BF16),
                   jax.ShapeDtypeStruct((2 * G, T // KEY_TILE, HEAD_DIM, KEY_TILE), BF16)],
        grid=(2 * G, T // tm),
        in_specs=[pl.BlockSpec((1, tm, LANE), slab(SLAB_K_SEL)), pl.BlockSpec((1, tm, LANE), slab(SLAB_V_SEL))],
        out_specs=[pl.BlockSpec((1, tm, LANE), lambda s, i: (s, i, 0)),
                   pl.BlockSpec((1, per_step, HEAD_DIM, KEY_TILE), lambda s, i: (s, i, 0, 0))],
        compiler_params=_params(("parallel", "parallel")),
        name="kv_prep",
    )(P, P)


def _active_tiles(act, S):
    return act[:, :, :, 0, :S // KEY_TILE].reshape(-1)


def _out_proj_kernel(ydn_ref, ynsa_ref, x_ref, wo_ref, g_ref, wrh_ref, wrl_ref, br_ref,
                     x1_ref, h2_ref, ri_ref, rw_ref, cnt_ref, carry_scr):
    tm = x_ref.shape[0]
    half = ynsa_ref.shape[1]

    @pl.when(pl.program_id(0) == 0)
    def _():
        carry_scr[...] = jnp.zeros_like(carry_scr)

    ydn = jnp.concatenate([ydn_ref[hh] for hh in range(ydn_ref.shape[0])], axis=1)
    x1 = x_ref[...] + _dot(ydn, wo_ref[0:half, :]) + _dot(ynsa_ref[...], wo_ref[half:, :])
    x1_ref[...] = x1
    h2 = x1 * lax.rsqrt(jnp.mean(x1 * x1, axis=-1, keepdims=True) + NORM_EPS) * g_ref[...]
    h2_ref[...] = h2

    hi = h2.astype(BF16)
    lo = (h2 - hi.astype(F32)).astype(BF16)
    lg = _dot(hi, wrh_ref[...]) + _dot(hi, wrl_ref[...]) + _dot(lo, wrh_ref[...])
    lane = lax.broadcasted_iota(I32, (1, LANE), 1)
    lanef = lane.astype(F32)
    biased = lg + br_ref[...]

    is_grp = lane < N_GROUPS
    gmax = jnp.max(jnp.where(is_grp, lg, NEG_INF), axis=-1, keepdims=True)
    eg = jnp.where(is_grp, jnp.exp(lg - gmax), 0.0)
    pg = eg / jnp.sum(eg, axis=-1, keepdims=True)
    gb = jnp.where(is_grp, biased, NEG_INF)
    g_sel = jnp.min(jnp.where(gb == jnp.max(gb, axis=-1, keepdims=True), lanef, float(LANE)),
                    axis=-1, keepdims=True)
    p_sel = _pick_lane(pg, lanef, g_sel)

    lo_lane = N_GROUPS + g_sel * EXPERTS_PER_GROUP
    in_grp = (lanef >= lo_lane) & (lanef < lo_lane + EXPERTS_PER_GROUP)
    v1 = jnp.where(in_grp, biased, NEG_INF)
    i1 = jnp.min(jnp.where(v1 == jnp.max(v1, axis=-1, keepdims=True), lanef, float(LANE)),
                 axis=-1, keepdims=True)
    v2 = jnp.where(lanef == i1, NEG_INF * 2, v1)
    i2 = jnp.min(jnp.where(v2 == jnp.max(v2, axis=-1, keepdims=True), lanef, float(LANE)),
                 axis=-1, keepdims=True)
    el1 = _pick_lane(lg, lanef, i1)
    el2 = _pick_lane(lg, lanef, i2)
    mx = jnp.maximum(el1, el2)
    e1 = jnp.exp(el1 - mx)
    e2 = jnp.exp(el2 - mx)
    rw_ref[...] = jnp.where(lane == 0, p_sel * (e1 / (e1 + e2)),
                            jnp.where(lane == 1, p_sel * (e2 / (e1 + e2)), 0.0))

    ex1 = i1 - N_GROUPS
    ex2 = i2 - N_GROUPS
    oh1 = lanef == ex1
    oh2 = lanef == ex2
    both = jnp.where(oh1 | oh2, 1.0, 0.0)
    ri = lax.broadcasted_iota(I32, (tm, tm), 0)
    ci = lax.broadcasted_iota(I32, (tm, tm), 1)
    before = _dot((ci < ri).astype(BF16), both.astype(BF16)) + carry_scr[...]
    r1 = jnp.sum(jnp.where(oh1, before, 0.0), axis=-1, keepdims=True)
    r2 = jnp.sum(jnp.where(oh2, before, 0.0), axis=-1, keepdims=True)
    packed = jnp.where(lane == 0, ex1, jnp.where(lane == 1, ex2,
                       jnp.where(lane == 2, r1, jnp.where(lane == 3, r2, 0.0))))
    ri_ref[...] = packed.astype(I32)
    total = carry_scr[...] + jnp.sum(both, axis=0, keepdims=True)
    carry_scr[...] = total
    cnt_ref[...] = jnp.broadcast_to(total, cnt_ref.shape).astype(I32)


def _out_proj(ydn, ynsa, x2d, wo, g, wr_hi, wr_lo, br, tm=256):
    T, D = x2d.shape
    half = ynsa.shape[1]
    row = lambda i: (i, 0)
    fixed = lambda i: (0, 0)
    return pl.pallas_call(
        _out_proj_kernel,
        out_shape=[
            jax.ShapeDtypeStruct((T, D), F32), jax.ShapeDtypeStruct((T, D), F32),
            jax.ShapeDtypeStruct((T, LANE), I32), jax.ShapeDtypeStruct((T, LANE), F32),
            jax.ShapeDtypeStruct((8, LANE), I32),
        ],
        grid=(T // tm,),
        in_specs=[
            pl.BlockSpec((ydn.shape[0], tm, LANE), lambda i: (0, i, 0)),
            pl.BlockSpec((tm, half), row), pl.BlockSpec((tm, D), row),
            pl.BlockSpec((D, D), fixed), pl.BlockSpec((1, D), fixed),
            pl.BlockSpec((D, LANE), fixed), pl.BlockSpec((D, LANE), fixed), pl.BlockSpec((1, LANE), fixed),
        ],
        out_specs=[
            pl.BlockSpec((tm, D), row), pl.BlockSpec((tm, D), row),
            pl.BlockSpec((tm, LANE), row), pl.BlockSpec((tm, LANE), row), pl.BlockSpec((8, LANE), fixed),
        ],
        scratch_shapes=[pltpu.VMEM((1, LANE), F32)],
        compiler_params=_params(("arbitrary",)),
        name="out_proj_router",
    )(ydn, ynsa, x2d, wo, g, wr_hi, wr_lo, br)


def _row_copy(src_hbm, row, dst, slot, sem):
    return pltpu.make_async_copy(src_hbm.at[pl.ds(row, 1)], dst.at[pl.ds(slot, 1)], sem)


GATHER_UNROLL = 8


def _gather_rows(src_hbm, idx_ref, base, dst, sem, n_rows):
    def issue(r, c):
        _row_copy(src_hbm, idx_ref[base + r], dst, r, sem).start()
        return c
    lax.fori_loop(0, n_rows, issue, 0, unroll=GATHER_UNROLL)


def _wait_rows(src_hbm, dst, sem):
    pltpu.make_async_copy(src_hbm.at[pl.ds(0, dst.shape[0])], dst, sem).wait()


def _moe_experts_kernel(be_ref, nu_ref, rt_ref, h_hbm, wg_ref, wu_ref, wd_ref, y_ref,
                        xs_a, xs_b, wg_scr, wu_scr, wd_scr, sem):
    i = pl.program_id(0)
    R = xs_a.shape[0]
    n_used = nu_ref[0]

    @pl.when((i == 0) & (n_used > 0))
    def _():
        _gather_rows(h_hbm, rt_ref, 0, xs_a, sem.at[0], R)

    @pl.when((i < n_used) & ((i == 0) | (be_ref[jnp.maximum(i - 1, 0)] != be_ref[i])))
    def _():
        wg_scr[...] = wg_ref[0].astype(BF16)
        wu_scr[...] = wu_ref[0].astype(BF16)
        wd_scr[...] = wd_ref[0].astype(BF16)

    def block(cur, nxt, s, prefetch):
        _wait_rows(h_hbm, cur, sem.at[s])
        if prefetch:
            for r in range(R):
                _row_copy(h_hbm, rt_ref[(i + 1) * R + r], nxt, r, sem.at[1 - s]).start()
        xb = cur[...].astype(BF16)
        hid = _silu(_dot(xb, wg_scr[...])) * _dot(xb, wu_scr[...])
        y_ref[...] = _dot(hid.astype(BF16), wd_scr[...])

    for s, (cur, nxt) in enumerate(((xs_a, xs_b), (xs_b, xs_a))):
        @pl.when((i % 2 == s) & (i + 1 < n_used))
        def _():
            block(cur, nxt, s, True)

        @pl.when((i % 2 == s) & (i + 1 == n_used))
        def _():
            block(cur, nxt, s, False)

    @pl.when(i >= n_used)
    def _():
        y_ref[...] = jnp.zeros_like(y_ref)


def _moe_experts(h2, block_expert, n_used, row_tok, w_gate, w_up, w_down):
    T, D = h2.shape
    R = MOE_ROWS
    n_blocks = block_expert.shape[0]
    DE = w_gate.shape[2]
    grid_spec = pltpu.PrefetchScalarGridSpec(
        num_scalar_prefetch=3,
        grid=(n_blocks,),
        in_specs=[
            pl.BlockSpec(memory_space=pl.ANY),
            pl.BlockSpec((1, D, DE), lambda i, be, nu, rt: (be[i], 0, 0)),
            pl.BlockSpec((1, D, DE), lambda i, be, nu, rt: (be[i], 0, 0)),
            pl.BlockSpec((1, DE, D), lambda i, be, nu, rt: (be[i], 0, 0)),
        ],
        out_specs=pl.BlockSpec((R, D), lambda i, be, nu, rt: (i, 0)),
        scratch_shapes=[
            pltpu.VMEM((R, D), F32), pltpu.VMEM((R, D), F32), pltpu.VMEM((D, DE), BF16),
            pltpu.VMEM((D, DE), BF16), pltpu.VMEM((DE, D), BF16), pltpu.SemaphoreType.DMA((2,)),
        ],
    )
    return pl.pallas_call(
        _moe_experts_kernel,
        out_shape=jax.ShapeDtypeStruct((n_blocks * R, D), F32),
        grid_spec=grid_spec,
        compiler_params=_params(("arbitrary",)),
        name="moe_experts",
    )(block_expert, n_used, row_tok, h2, w_gate, w_up, w_down)


def _moe_combine_kernel(dest_ref, x1_ref, rw_ref, g_ref, ys_hbm, o_ref, buf_a, buf_b, sem):
    i = pl.program_id(0)
    n_steps = pl.num_programs(0)
    tm = x1_ref.shape[0]
    T = tm * n_steps

    @pl.when(i == 0)
    def _():
        for k in range(2):
            _gather_rows(ys_hbm, dest_ref, k * T, buf_a.at[k], sem.at[0], tm)

    def step(cur, nxt, s, prefetch):
        for k in range(2):
            _wait_rows(ys_hbm, cur.at[k], sem.at[s])
        if prefetch:
            for k in range(2):
                for r in range(tm):
                    _row_copy(ys_hbm, dest_ref[k * T + (i + 1) * tm + r], nxt.at[k], r,
                              sem.at[1 - s]).start(priority=r % 2)
        rw = rw_ref[...]
        x2 = x1_ref[...] + cur[0] * rw[:, 0:1] + cur[1] * rw[:, 1:2]
        o_ref[...] = x2 * lax.rsqrt(jnp.mean(x2 * x2, axis=-1, keepdims=True) + NORM_EPS) * g_ref[...]

    for s, (cur, nxt) in enumerate(((buf_a, buf_b), (buf_b, buf_a))):
        @pl.when((i % 2 == s) & (i + 1 < n_steps))
        def _():
            step(cur, nxt, s, True)

        @pl.when((i % 2 == s) & (i + 1 == n_steps))
        def _():
            step(cur, nxt, s, False)


def _moe_combine(dest_flat, x1, rw, g, ys, tm=128):
    T, D = x1.shape
    grid_spec = pltpu.PrefetchScalarGridSpec(
        num_scalar_prefetch=1,
        grid=(T // tm,),
        in_specs=[
            pl.BlockSpec((tm, D), lambda i, d: (i, 0)),
            pl.BlockSpec((tm, LANE), lambda i, d: (i, 0)),
            pl.BlockSpec((1, D), lambda i, d: (0, 0)),
            pl.BlockSpec(memory_space=pl.ANY),
        ],
        out_specs=pl.BlockSpec((tm, D), lambda i, d: (i, 0)),
        scratch_shapes=[pltpu.VMEM((2, tm, D), F32), pltpu.VMEM((2, tm, D), F32),
                        pltpu.SemaphoreType.DMA((2,))],
    )
    return pl.pallas_call(
        _moe_combine_kernel,
        out_shape=jax.ShapeDtypeStruct((T, D), F32),
        grid_spec=grid_spec,
        compiler_params=_params(("arbitrary",)),
        name="moe_combine",
    )(dest_flat, x1, rw, g, ys)


def _slab_weights(w_in):
    D = w_in.shape[0]
    n_dn = 4 * DN_HEADS * HEAD_DIM
    n_small_a = 2 * DN_HEADS
    n_nsa = NSA_HEADS * HEAD_DIM + 6 * NSA_KV_GROUPS * HEAD_DIM
    c0 = n_dn + n_small_a
    c1 = c0 + n_nsa
    n_gate = 3 * NSA_HEADS
    pad = N_SLABS * LANE - (n_dn + n_nsa + n_small_a + n_gate)
    return jnp.concatenate(
        [w_in[:, :n_dn], w_in[:, c0:c1], w_in[:, n_dn:c0], w_in[:, c1:c1 + n_gate],
         jnp.zeros((D, pad), w_in.dtype)], axis=1).astype(BF16)


def _layer(x, norm_mix_g, w_in, conv_w, dt_bias, a_log, dn_norm_g, cmp_pos_k, cmp_w1_k, cmp_w2_k,
           cmp_pos_v, cmp_w1_v, cmp_w2_v, nsa_norm_g, w_out, norm_ffn_g, w_group, b_group,
           w_router, b_router, w_gate, w_up, w_down, norm_final_g):
    B, S, D = x.shape
    T = B * S
    x2d = x.reshape(T, D)

    P = _in_proj(x2d, norm_mix_g.reshape(1, D), _slab_weights(w_in))
    ydn = _deltanet(P, B, S, conv_w, dt_bias, a_log, dn_norm_g)

    kvc, kvct = _compress(P, B, S, jnp.stack([cmp_pos_k, cmp_pos_v]), jnp.stack([cmp_w1_k, cmp_w1_v]),
                          jnp.stack([cmp_w2_k, cmp_w2_v]))
    oct, selt, act = _nsa_select(P, kvc, kvct, B, S)
    ynsa = _nsa_attend(P, *_attend_operands(P, B, S), selt, _active_tiles(act, S), oct, B, S, nsa_norm_g)

    wr = jnp.zeros((D, LANE), F32).at[:, :N_GROUPS].set(w_group).at[:, N_GROUPS:N_GROUPS + N_EXPERTS].set(w_router)
    wr_hi = wr.astype(BF16)
    wr_lo = (wr - wr_hi.astype(F32)).astype(BF16)
    br = jnp.zeros((1, LANE), F32).at[0, :N_GROUPS].set(b_group).at[0, N_GROUPS:N_GROUPS + N_EXPERTS].set(b_router)
    x1, h2, ri, rw, cnt = _out_proj(ydn, ynsa, x2d, w_out.astype(BF16), norm_ffn_g.reshape(1, D),
                                    wr_hi, wr_lo, br)

    R = MOE_ROWS
    counts = cnt[0, :N_EXPERTS]
    padded = (counts + R - 1) // R * R
    pends = jnp.cumsum(padded)
    pstarts = pends - padded
    first_row = jnp.sum(jnp.where(ri[:, 0:2, None] == jnp.arange(N_EXPERTS, dtype=I32), pstarts, 0), axis=-1)
    dest = first_row + ri[:, 2:4]
    dest_flat = dest.T.reshape(-1).astype(I32)
    n_blocks = (2 * T + R - 1) // R + N_EXPERTS
    block_start = jnp.arange(n_blocks, dtype=I32) * R
    block_expert = jnp.minimum(jnp.sum(block_start[:, None] >= pends[None, :], axis=1),
                               N_EXPERTS - 1).astype(I32)
    n_used = (pends[-1:] // R).astype(I32)
    row_tok = jnp.zeros((n_blocks * R,), I32).at[dest_flat].set(jnp.tile(jnp.arange(T, dtype=I32), 2))

    ys = _moe_experts(h2, block_expert, n_used, row_tok, w_gate, w_up, w_down)
    out = _moe_combine(dest_flat, x1, rw, norm_final_g.reshape(1, D), ys)
    return out.reshape(B, S, D)


def kernel(x, norm_mix_g, w_in, conv_w, dt_bias, a_log, dn_norm_g, cmp_pos_k, cmp_w1_k, cmp_w2_k,
           cmp_pos_v, cmp_w1_v, cmp_w2_v, nsa_norm_g, w_out, norm_ffn_g, w_group, b_group,
           w_router, b_router, w_gate, w_up, w_down, norm_final_g):
    assert w_in.shape[0] == 1, "one layer"
    return _layer(x, norm_mix_g[0], w_in[0], conv_w[0], dt_bias[0], a_log[0], dn_norm_g[0],
                  cmp_pos_k[0], cmp_w1_k[0], cmp_w2_k[0], cmp_pos_v[0], cmp_w1_v[0], cmp_w2_v[0],
                  nsa_norm_g[0], w_out[0], norm_ffn_g[0], w_group[0], b_group[0], w_router[0],
                  b_router[0], w_gate[0], w_up[0], w_down[0], norm_final_g)
```
